```python
import jax, jax.numpy as jnp
from jax import lax
import numpy as np

D_MODEL = 1024
BATCH = 16
SEQ = 2048
DEPTH = 1
DEC_BATCH = 32
DEC_SEQ = 1
PAST_LEN = 16384
PAGE_SIZE = 128

H_RET = 8
DK_RET = 64
DV_RET = 64
RET_WIDTH = H_RET * DV_RET
RET_CHUNK = 128
RET_ROPE_BASE = 10000.0
H_ATT = 8
HD_ATT = 64
ATT_WIDTH = H_ATT * HD_ATT
ATT_ROPE_THETA = 500000.0
ROT_DIMS = HD_ATT // 4
DILATIONS = ((128, 1), (512, 4), (2048, 16))
WINDOW_MAX = 2048
Q_BLOCK = 128
MIX_WIDTH = RET_WIDTH + ATT_WIDTH
PROJ_WIDTH = 4 * RET_WIDTH + 3 * ATT_WIDTH
PROJ_SPLITS = (RET_WIDTH, 2 * RET_WIDTH, 3 * RET_WIDTH, 4 * RET_WIDTH,
               4 * RET_WIDTH + ATT_WIDTH, 4 * RET_WIDTH + 2 * ATT_WIDTH)
D_FF = 2816
CONV_W = 3
PLE_DIM = 256
LN_EPS = 1e-5
ALPHA = (2 * DEPTH) ** 0.25
BETA = (8 * DEPTH) ** -0.25

kernel_name = "hymba_retnet_dilated_convffn_step"


def layer_norm(x, g, b):
    xf = x.astype(jnp.float32)
    mu = jnp.mean(xf, axis=-1, keepdims=True)
    var = jnp.mean(jnp.square(xf - mu), axis=-1, keepdims=True)
    return ((xf - mu) * lax.rsqrt(var + LN_EPS)).astype(x.dtype) * g + b


def head_norm(x):
    mu = jnp.mean(x, axis=-1, keepdims=True)
    var = jnp.mean(jnp.square(x - mu), axis=-1, keepdims=True)
    return (x - mu) * lax.rsqrt(var + LN_EPS)


def rotary(x, pos, n_rot, base):
    half = n_rot // 2
    inv = base ** (-jnp.arange(half, dtype=jnp.float32) / half)
    ang = pos.astype(jnp.float32)[:, None] * inv[None, :]
    cos = jnp.cos(ang)[:, None, :]
    sin = jnp.sin(ang)[:, None, :]
    xr = x[..., :n_rot].astype(jnp.float32)
    x1, x2 = xr[..., :half], xr[..., half:]
    rot = jnp.concatenate([x1 * cos - x2 * sin, x2 * cos + x1 * sin], axis=-1).astype(x.dtype)
    return jnp.concatenate([rot, x[..., n_rot:]], axis=-1)


def project(h, w_in, pos):
    B, T, _ = h.shape
    q_r, k_r, v_r, g_r, q_a, k_a, v_a = jnp.split(h @ w_in, PROJ_SPLITS, axis=-1)
    q_r = rotary(q_r.reshape(B, T, H_RET, DK_RET), pos, DK_RET, RET_ROPE_BASE)
    k_r = rotary(k_r.reshape(B, T, H_RET, DK_RET), pos, DK_RET, RET_ROPE_BASE) * (DK_RET ** -0.5)
    v_r = v_r.reshape(B, T, H_RET, DV_RET)
    q_a = rotary(q_a.reshape(B, T, H_ATT, HD_ATT), pos, ROT_DIMS, ATT_ROPE_THETA)
    k_a = rotary(k_a.reshape(B, T, H_ATT, HD_ATT), pos, ROT_DIMS, ATT_ROPE_THETA)
    v_a = v_a.reshape(B, T, H_ATT, HD_ATT)
    return q_r, k_r, v_r, g_r, q_a, k_a, v_a


def _log_gamma():
    return jnp.log(1.0 - 2.0 ** (-5.0 - jnp.arange(H_RET, dtype=jnp.float32)))


def retention_chunk(state, qkv):
    q, k, v = qkv
    C = q.shape[1]
    lg = _log_gamma()
    idx = jnp.arange(C, dtype=jnp.float32)
    rel = idx[:, None] - idx[None, :]
    decay = jnp.where(rel >= 0, jnp.exp(lg[:, None, None] * jnp.maximum(rel, 0.0)), 0.0)
    inner = jnp.einsum('bhij,bjhe->bihe', jnp.einsum('bihd,bjhd->bhij', q, k) * decay, v)
    cross = jnp.einsum('bihd,bhde->bihe', q, state) * jnp.exp(lg[None, :] * (idx[:, None] + 1.0))[None, :, :, None]
    k_dec = k * jnp.exp(lg[None, :] * (C - 1.0 - idx[:, None]))[None, :, :, None]
    new_state = jnp.exp(lg * C)[None, :, None, None] * state + jnp.einsum('bjhd,bjhe->bhde', k_dec, v)
    return new_state, inner + cross


def retention_prompt(q, k, v):
    B, S, H, _ = q.shape
    n = S // RET_CHUNK

    def chunks(t):
        return jnp.moveaxis(t.astype(jnp.float32).reshape(B, n, RET_CHUNK, H, t.shape[-1]), 1, 0)

    s0 = jnp.zeros((B, H, DK_RET, DV_RET), jnp.float32)
    fin, o = lax.scan(retention_chunk, s0, (chunks(q), chunks(k), chunks(v)))
    return jnp.moveaxis(o, 0, 1).reshape(B, S, H, DV_RET), fin


def _masked_softmax_stats(s, valid):
    s = jnp.where(valid, s, -jnp.inf)
    m = jnp.max(s, axis=-1, keepdims=True)
    pr = jnp.exp(s - m)
    den = jnp.sum(pr, axis=-1, keepdims=True)
    return pr / den, (m + jnp.log(den))[..., 0]


def _to_strided(x, d):
    B, S, H, E = x.shape
    return x.reshape(B, S // d, d, H, E).transpose(0, 2, 1, 3, 4).reshape(B * d, S // d, H, E)


def _from_strided(x, B, d):
    L = x.shape[1]
    rest = x.shape[2:]
    x = jnp.moveaxis(x.reshape((B, d, L) + rest), 1, 2)
    return x.reshape((B, L * d) + rest)


def _banded_dilated(q, k, v, span, d):
    B, S, H, E = q.shape
    L = S // d
    nb = -(-L // Q_BLOCK)
    Lp = nb * Q_BLOCK
    qs = jnp.pad(_to_strided(q, d), ((0, 0), (0, Lp - L), (0, 0), (0, 0))).reshape(B * d, nb, Q_BLOCK, H, E)

    def band(t):
        t = jnp.pad(_to_strided(t, d), ((0, 0), (Q_BLOCK, Lp - L), (0, 0), (0, 0)))
        t = t.reshape(B * d, nb + 1, Q_BLOCK, H, E)
        return jnp.concatenate([t[:, :-1], t[:, 1:]], axis=2)

    kw, vw = band(k), band(v)
    s = jnp.einsum('znqhe,znkhe->znhqk', qs, kw, preferred_element_type=jnp.float32) * (E ** -0.5)
    qi = jnp.arange(Q_BLOCK)[:, None]
    ki = jnp.arange(2 * Q_BLOCK)[None, :]
    dist = Q_BLOCK + qi - ki
    kpos = jnp.arange(nb)[:, None, None] * Q_BLOCK + ki[None] - Q_BLOCK
    valid = ((dist >= 0) & (dist <= span))[None] & (kpos >= 0)
    pr, lse = _masked_softmax_stats(s, valid[None, :, None])
    o = jnp.einsum('znhqk,znkhe->znqhe', pr, vw.astype(jnp.float32)).reshape(B * d, Lp, H, E)[:, :L]
    lse = jnp.swapaxes(lse, 2, 3).reshape(B * d, Lp, H)[:, :L]
    return _from_strided(o, B, d), _from_strided(lse, B, d)


def _gathered_dilated(q, k_all, v_all, span, d, n_past):
    E = q.shape[-1]
    T = q.shape[1]
    idx = n_past + jnp.arange(T)[:, None] - d * jnp.arange(span + 1)[None, :]
    valid = idx >= 0
    idc = jnp.clip(idx, 0)
    kg = k_all[:, idc]
    vg = v_all[:, idc]
    s = jnp.einsum('bthe,btjhe->bthj', q, kg, preferred_element_type=jnp.float32) * (E ** -0.5)
    pr, lse = _masked_softmax_stats(s, valid[None, :, None, :])
    return jnp.einsum('bthj,btjhe->bthe', pr, vg.astype(jnp.float32)), lse


def combine_by_denominator(outs, lses):
    w = jax.nn.softmax(jnp.stack(lses, axis=0), axis=0)
    return jnp.einsum('pbth,pbthe->bthe', w, jnp.stack(outs, axis=0))


def dilated_attention_prompt(q, k, v):
    outs, lses = [], []
    for window, dil in DILATIONS:
        o, l = _banded_dilated(q, k, v, window // dil, dil)
        outs.append(o)
        lses.append(l)
    return combine_by_denominator(outs, lses).astype(q.dtype)


def dilated_attention_sample(q, k_new, v_new, k_buf, v_buf):
    n_past = k_buf.shape[1]
    k_all = jnp.concatenate([k_buf.astype(k_new.dtype), k_new], axis=1)
    v_all = jnp.concatenate([v_buf.astype(v_new.dtype), v_new], axis=1)
    outs, lses = [], []
    for window, dil in DILATIONS:
        o, l = _gathered_dilated(q, k_all, v_all, window // dil, dil, n_past)
        outs.append(o)
        lses.append(l)
    return combine_by_denominator(outs, lses).astype(q.dtype)


def merge_heads(ret_o, g_r, att_o, w_out):
    B, T = ret_o.shape[:2]
    r = head_norm(ret_o).reshape(B, T, RET_WIDTH).astype(g_r.dtype) * jax.nn.silu(g_r)
    a = att_o.reshape(B, T, ATT_WIDTH)
    return jnp.concatenate([r, a], axis=-1) @ w_out


def conv_ffn(h, prefix, w_up, conv_w, conv_b, w_down):
    T = h.shape[1]
    u = h @ w_up
    up = jnp.concatenate([prefix.astype(u.dtype), u], axis=1)
    c = conv_b + sum(conv_w[j] * up[:, j:j + T] for j in range(CONV_W))
    a, g = jnp.split(c, 2, axis=-1)
    return (a * jax.nn.gelu(g)) @ w_down, up[:, T:]


def channel_and_embed(h, mix, p, prefix, ln1_g, ln1_b, w_up, conv_w, conv_b, w_down,
                      w_ple_gate, w_ple_proj, ln2_g, ln2_b):
    h1 = layer_norm(ALPHA * h + mix, ln1_g, ln1_b)
    f, conv_state = conv_ffn(h1, prefix, w_up, conv_w, conv_b, w_down)
    e = jax.nn.sigmoid(h1 @ w_ple_gate) * (p @ w_ple_proj)
    return layer_norm(ALPHA * h1 + f + e, ln2_g, ln2_b), conv_state


def setup_inputs(seed: int = 0) -> dict:
    key = jax.random.key(seed)
    ks = jax.random.split(key, 22)
    f32 = jnp.float32
    win_buf = min(WINDOW_MAX, PAST_LEN)
    two_f = 2 * D_FF

    def nrm(k, shape, scale):
        return jax.random.normal(k, shape, f32) * scale

    return {
        'x_prompt': nrm(ks[0], (BATCH, SEQ, D_MODEL), 1.0),
        'x_sample': nrm(ks[1], (DEC_BATCH, DEC_SEQ, D_MODEL), 1.0),
        'cache_k_win': nrm(ks[2], (DEPTH, DEC_BATCH, win_buf, H_ATT, HD_ATT), 1.0),
        'cache_v_win': nrm(ks[3], (DEPTH, DEC_BATCH, win_buf, H_ATT, HD_ATT), 1.0),
        'state_ret': nrm(ks[4], (DEPTH, DEC_BATCH, H_RET, DK_RET, DV_RET), 0.5),
        'state_conv': nrm(ks[5], (DEPTH, DEC_BATCH, CONV_W - 1, two_f), 1.0),
        'p_prompt': nrm(ks[6], (DEPTH, BATCH, SEQ, PLE_DIM), 1.0),
        'p_sample': nrm(ks[7], (DEPTH, DEC_BATCH, DEC_SEQ, PLE_DIM), 1.0),
        'ln_in_g': 1.0 + nrm(ks[8], (D_MODEL,), 0.02),
        'ln_in_b': nrm(ks[9], (D_MODEL,), 0.02),
        'w_in': nrm(ks[10], (DEPTH, D_MODEL, PROJ_WIDTH), D_MODEL ** -0.5),
        'w_out': nrm(ks[11], (DEPTH, MIX_WIDTH, D_MODEL), BETA * MIX_WIDTH ** -0.5),
        'ln1_g': 1.0 + nrm(ks[12], (DEPTH, D_MODEL), 0.02),
        'ln1_b': nrm(ks[13], (DEPTH, D_MODEL), 0.02),
        'w_up': nrm(ks[14], (DEPTH, D_MODEL, two_f), D_MODEL ** -0.5),
        'conv_w': nrm(ks[15], (DEPTH, CONV_W, two_f), CONV_W ** -0.5),
        'conv_b': nrm(ks[16], (DEPTH, two_f), 0.02),
        'w_down': nrm(ks[17], (DEPTH, D_FF, D_MODEL), BETA * D_FF ** -0.5),
        'w_ple_gate': nrm(ks[18], (DEPTH, D_MODEL, D_MODEL), D_MODEL ** -0.5),
        'w_ple_proj': nrm(ks[19], (DEPTH, PLE_DIM, D_MODEL), BETA * PLE_DIM ** -0.5),
        'ln2_g': 1.0 + nrm(ks[20], (DEPTH, D_MODEL), 0.02),
        'ln2_b': nrm(ks[21], (DEPTH, D_MODEL), 0.02),
    }


def reference(x_prompt, x_sample, cache_k_win, cache_v_win, state_ret, state_conv, p_prompt, p_sample,
              ln_in_g, ln_in_b, w_in, w_out, ln1_g, ln1_b, w_up, conv_w, conv_b, w_down,
              w_ple_gate, w_ple_proj, ln2_g, ln2_b):
    B, S, _ = x_prompt.shape
    T = x_sample.shape[1]
    pos_p = jnp.arange(S, dtype=jnp.int32)
    pos_s = PAST_LEN + jnp.arange(T, dtype=jnp.int32)
    keep = min(WINDOW_MAX, S)
    hp = layer_norm(x_prompt, ln_in_g, ln_in_b)
    hs = layer_norm(x_sample, ln_in_g, ln_in_b)
    kp_l, vp_l, rp_l, cp_l, ks_l, vs_l, rs_l, cs_l = [], [], [], [], [], [], [], []
    for i in range(DEPTH):
        q_r, k_r, v_r, g_r, q_a, k_a, v_a = project(hp, w_in[i], pos_p)
        ret_o, ret_fin = retention_prompt(q_r, k_r, v_r)
        att_o = dilated_attention_prompt(q_a, k_a, v_a)
        mix = merge_heads(ret_o, g_r, att_o, w_out[i])
        hp, conv_p = channel_and_embed(hp, mix, p_prompt[i], jnp.zeros((B, CONV_W - 1, 2 * D_FF), hp.dtype),
                                       ln1_g[i], ln1_b[i], w_up[i], conv_w[i], conv_b[i], w_down[i],
                                       w_ple_gate[i], w_ple_proj[i], ln2_g[i], ln2_b[i])
        kp_l.append(k_a[:, S - keep:])
        vp_l.append(v_a[:, S - keep:])
        rp_l.append(ret_fin.astype(state_ret.dtype))
        cp_l.append(conv_p)
        q_r, k_r, v_r, g_r, q_a, k_a, v_a = project(hs, w_in[i], pos_s)
        ret_new, ret_o = retention_chunk(state_ret[i].astype(jnp.float32),
                                         (q_r.astype(jnp.float32), k_r.astype(jnp.float32), v_r.astype(jnp.float32)))
        att_o = dilated_attention_sample(q_a, k_a, v_a, cache_k_win[i], cache_v_win[i])
        mix = merge_heads(ret_o, g_r, att_o, w_out[i])
        hs, conv_s = channel_and_embed(hs, mix, p_sample[i], state_conv[i],
                                       ln1_g[i], ln1_b[i], w_up[i], conv_w[i], conv_b[i], w_down[i],
                                       w_ple_gate[i], w_ple_proj[i], ln2_g[i], ln2_b[i])
        ks_l.append(k_a)
        vs_l.append(v_a)
        rs_l.append(ret_new.astype(state_ret.dtype))
        cs_l.append(conv_s)
    return (hp, hs, jnp.stack(kp_l), jnp.stack(vp_l), jnp.stack(rp_l), jnp.stack(cp_l),
            jnp.stack(ks_l), jnp.stack(vs_l), jnp.stack(rs_l), jnp.stack(cs_l))
```

```python
import functools
import math

import numpy as np
import jax
import jax.numpy as jnp
from jax import lax
from jax.experimental import pallas as pl
from jax.experimental.pallas import tpu as pltpu

F32 = jnp.float32
BF16 = jnp.bfloat16

D_MODEL = 1024
PAST_LEN = 16384
H_RET = 8
DK_RET = 64
RET_WIDTH = 512
RET_CHUNK = 128
RET_ROPE_BASE = 10000.0
H_ATT = 8
HD_ATT = 64
ATT_WIDTH = 512
ATT_ROPE_THETA = 500000.0
ROT_DIMS = HD_ATT // 4
DILATIONS = ((128, 1), (512, 4), (2048, 16))
WINDOW_MAX = 2048
Q_BLOCK = 128
PROJ_WIDTH = 4 * RET_WIDTH + 3 * ATT_WIDTH
D_FF = 2816
CONV_W = 3
PLE_DIM = 256
LN_EPS = 1e-5
DEPTH = 1
ALPHA = (2 * DEPTH) ** 0.25

LANES = 128
HEADS_PER_TILE = LANES // DK_RET
NEG_BIG = -1e30
LOG_GAMMA = tuple(math.log(1.0 - 2.0 ** (-5.0 - h)) for h in range(H_RET))
ATT_SCALE = HD_ATT ** -0.5
VMEM_LIMIT = 56 * 1024 * 1024

PROJ_TILE = 512
POST_TILE = 512
FF_CHUNK = 256


def _layer_norm(x, g, b):
    mu = jnp.mean(x, axis=-1, keepdims=True)
    xc = x - mu
    var = jnp.mean(xc * xc, axis=-1, keepdims=True)
    return xc * lax.rsqrt(var + LN_EPS) * g + b


def _sigmoid(x):
    return 1.0 / (1.0 + jnp.exp(-x))


def _gelu_tanh(x):
    return 0.5 * x * (1.0 + jnp.tanh(math.sqrt(2.0 / math.pi) * (x + 0.044715 * (x * x * x))))


def _select_log_gamma(head_idx):
    out = jnp.zeros(head_idx.shape, F32)
    for h in range(H_RET):
        out = jnp.where(head_idx == h, LOG_GAMMA[h], out)
    return out


def _const_spec(shape):
    nd = len(shape)
    return pl.BlockSpec(shape, lambda *_: (0,) * nd, pipeline_mode=pl.Buffered(1))


def _rotate(z, cos, s_lo, s_hi, half):
    up = pltpu.roll(z, LANES - half, 1)
    dn = pltpu.roll(z, half, 1)
    return z * cos + up * s_lo + dn * s_hi


def _proj_kernel(x_ref, g_ref, b_ref, w_ref, cr_ref, slr_ref, shr_ref, ca_ref, sla_ref, sha_ref,
                 qr_ref, kr_ref, vr_ref, gr_ref, qa_ref, ka_ref, va_ref):
    hb = _layer_norm(x_ref[...], g_ref[...], b_ref[...]).astype(BF16)
    outs = (qr_ref, kr_ref, vr_ref, gr_ref, qa_ref, ka_ref, va_ref)
    for grp, o_ref in enumerate(outs):
        for j in range(RET_WIDTH // LANES):
            c0 = grp * RET_WIDTH + j * LANES
            z = jnp.dot(hb, w_ref[:, c0:c0 + LANES], preferred_element_type=F32)
            if grp in (0, 1):
                z = _rotate(z, cr_ref[...], slr_ref[...], shr_ref[...], DK_RET // 2)
                if grp == 1:
                    z = z * (DK_RET ** -0.5)
            elif grp in (4, 5):
                z = _rotate(z, ca_ref[...], sla_ref[...], sha_ref[...], ROT_DIMS // 2)
            o_ref[:, j * LANES:(j + 1) * LANES] = z.astype(o_ref.dtype)


def _rot_tables(pos, n_rot, base):
    half = n_rot // 2
    inv = base ** (-jnp.arange(half, dtype=F32) / half)
    ang = pos.astype(F32)[:, None] * inv[None, :]
    cos, sin = jnp.cos(ang), jnp.sin(ang)
    l = np.arange(LANES) % DK_RET
    idx = np.where(l < half, l, np.where(l < n_rot, l - half, 0))
    lo = jnp.asarray(l < half)
    hi = jnp.asarray((l >= half) & (l < n_rot))
    cos_f = jnp.where(jnp.asarray(l < n_rot), cos[:, idx], 1.0)
    s_lo = jnp.where(lo, -sin[:, idx], 0.0)
    s_hi = jnp.where(hi, sin[:, idx], 0.0)
    return cos_f, s_lo, s_hi


def _proj_call(x2d, ln_g, ln_b, w_in_b, tabs, tm, tab_blocks, out_dtypes):
    n = x2d.shape[0]
    row = lambda i: (i, 0)
    tab = lambda i: (i % tab_blocks, 0)
    tab_spec = pl.BlockSpec((tm, LANES), tab)
    return pl.pallas_call(
        _proj_kernel,
        grid=(n // tm,),
        in_specs=[pl.BlockSpec((tm, D_MODEL), row), _const_spec((1, D_MODEL)), _const_spec((1, D_MODEL)),
                  _const_spec((D_MODEL, PROJ_WIDTH))] + [tab_spec] * 6,
        out_specs=[pl.BlockSpec((tm, RET_WIDTH), row)] * 7,
        out_shape=[jax.ShapeDtypeStruct((n, RET_WIDTH), dt) for dt in out_dtypes],
        compiler_params=pltpu.CompilerParams(dimension_semantics=("parallel",), vmem_limit_bytes=VMEM_LIMIT),
    )(x2d, ln_g, ln_b, w_in_b, *tabs)


def _ret_kernel(q_ref, k_ref, v_ref, g_ref, r_ref, st_ref, *, n_chunks):
    C = RET_CHUNK
    hp = pl.program_id(1)
    lane1 = lax.broadcasted_iota(jnp.int32, (1, LANES), 1)
    lg_lane = _select_log_gamma(HEADS_PER_TILE * hp + (lane1 >= DK_RET).astype(jnp.int32))

    ri = lax.broadcasted_iota(jnp.int32, (2 * C, C), 0)
    ci = lax.broadcasted_iota(jnp.int32, (2 * C, C), 1)
    rel = (jnp.where(ri >= C, ri - C, ri) - ci).astype(F32)
    lg_rows = _select_log_gamma(HEADS_PER_TILE * hp + (ri >= C).astype(jnp.int32))
    decay = jnp.where(rel >= 0, jnp.exp(lg_rows * jnp.maximum(rel, 0.0)), 0.0)

    tok = lax.broadcasted_iota(jnp.int32, (C, LANES), 0).astype(F32)
    cross_dec = jnp.exp(lg_lane * (tok + 1.0))
    k_dec = jnp.exp(lg_lane * (C - 1.0 - tok))
    sr = lax.broadcasted_iota(jnp.int32, (LANES, LANES), 0)
    sc = lax.broadcasted_iota(jnp.int32, (LANES, LANES), 1)
    same_head = ((sr >= DK_RET) == (sc >= DK_RET)).astype(F32)
    state_dec = jnp.exp(_select_log_gamma(HEADS_PER_TILE * hp + (sr >= DK_RET).astype(jnp.int32)) * float(C))

    def body(c, state):
        first = lax.broadcasted_iota(jnp.int32, (1, LANES), 1) < DK_RET
        off = pl.multiple_of(c * C, C)
        q = q_ref[pl.ds(off, C), :]
        k = k_ref[pl.ds(off, C), :]
        v = v_ref[pl.ds(off, C), :]
        g = g_ref[pl.ds(off, C), :]
        zero = jnp.zeros_like(q)
        qs = jnp.concatenate([jnp.where(first, q, zero), jnp.where(first, zero, q)], axis=0)
        s = lax.dot_general(qs, k.astype(BF16), (((1,), (1,)), ((), ())), preferred_element_type=F32)
        pv = jnp.dot((s * decay).astype(BF16), v, preferred_element_type=F32)
        inner = jnp.where(first, pv[:C], pv[C:])
        cross = jnp.dot(q, state.astype(BF16), preferred_element_type=F32) * cross_dec
        o = inner + cross
        upd = lax.dot_general((k * k_dec).astype(BF16), v, (((0,), (0,)), ((), ())),
                              preferred_element_type=F32)
        new_state = state_dec * state + same_head * upd
        s_a = jnp.sum(jnp.where(first, o, 0.0), axis=-1, keepdims=True)
        s_b = jnp.sum(jnp.where(first, 0.0, o), axis=-1, keepdims=True)
        xc = o - jnp.where(first, s_a, s_b) * (1.0 / DK_RET)
        sq = xc * xc
        v_a = jnp.sum(jnp.where(first, sq, 0.0), axis=-1, keepdims=True)
        v_b = jnp.sum(jnp.where(first, 0.0, sq), axis=-1, keepdims=True)
        rn = xc * lax.rsqrt(jnp.where(first, v_a, v_b) * (1.0 / DK_RET) + LN_EPS)
        r_ref[pl.ds(off, C), :] = (rn * (g * _sigmoid(g))).astype(r_ref.dtype)
        return new_state

    state = lax.fori_loop(0, n_chunks, body, jnp.zeros((LANES, LANES), F32))
    st_ref[0, 0] = state[:DK_RET, :DK_RET]
    st_ref[0, 1] = state[DK_RET:, DK_RET:]


def _ret_call(qr, kr, vr, gr, batch, seq):
    blk = pl.BlockSpec((seq, LANES), lambda b, p: (b, p))
    n_pairs = H_RET // HEADS_PER_TILE
    return pl.pallas_call(
        functools.partial(_ret_kernel, n_chunks=seq // RET_CHUNK),
        grid=(batch, n_pairs),
        in_specs=[blk] * 4,
        out_specs=[blk, pl.BlockSpec((1, HEADS_PER_TILE, DK_RET, DK_RET), lambda b, p: (b, p, 0, 0))],
        out_shape=[jax.ShapeDtypeStruct((batch * seq, RET_WIDTH), BF16),
                   jax.ShapeDtypeStruct((batch, H_RET, DK_RET, DK_RET), F32)],
        compiler_params=pltpu.CompilerParams(dimension_semantics=("parallel", "parallel"),
                                             vmem_limit_bytes=VMEM_LIMIT),
    )(qr, kr, vr, gr)


def _att_kernel(q_ref, k_ref, v_ref, o_ref, acc_ref, lse_ref, *, seq):
    QB = Q_BLOCK
    ri = lax.broadcasted_iota(jnp.int32, (2 * QB, 2 * QB), 0)
    ci = lax.broadcasted_iota(jnp.int32, (2 * QB, 2 * QB), 1)
    qi = jnp.where(ri >= QB, ri - QB, ri)
    valid_band = ((ci < QB) & (ci >= qi)) | ((ci >= QB) & (ci - QB <= qi))
    bias_band = jnp.where(valid_band, 0.0, NEG_BIG)
    rd = lax.broadcasted_iota(jnp.int32, (2 * QB, QB), 0)
    cd = lax.broadcasted_iota(jnp.int32, (2 * QB, QB), 1)
    bias_diag = jnp.where(cd <= jnp.where(rd >= QB, rd - QB, rd), 0.0, NEG_BIG)
    ones = jnp.ones((2 * QB, LANES), BF16)

    def unit(br, dil, q_start, k_start, n_keys, bias):
        first = lax.broadcasted_iota(jnp.int32, (1, LANES), 1) < HD_ATT
        rows_q = pl.ds(q_start, QB, stride=dil) if dil > 1 else pl.ds(q_start, QB)
        rows_k = pl.ds(k_start, n_keys, stride=dil) if dil > 1 else pl.ds(k_start, n_keys)
        q = q_ref[rows_q, :] * ATT_SCALE
        qs = jnp.concatenate([jnp.where(first, q, 0.0), jnp.where(first, 0.0, q)], axis=0).astype(BF16)
        kk = k_ref[rows_k, :].astype(BF16)
        vv = v_ref[rows_k, :].astype(BF16)
        s = lax.dot_general(qs, kk, (((1,), (1,)), ((), ())), preferred_element_type=F32)
        s = s + bias
        m = jnp.max(s, axis=-1, keepdims=True)
        p = jnp.exp(s - m).astype(BF16)
        pv = jnp.dot(p, jnp.concatenate([vv, ones[:n_keys]], axis=1), preferred_element_type=F32)
        den = pv[:, LANES:]
        o = pv[:, :LANES] / den
        lse = m + jnp.log(den)
        acc_ref[br, rows_q, :] = jnp.where(first, o[:QB], o[QB:])
        lse_ref[br, rows_q, :] = jnp.where(first, lse[:QB], lse[QB:])

    for br, (window, dil) in enumerate(DILATIONS):
        assert window // dil == QB
        sub_len = seq // dil
        n_blocks = sub_len // QB

        def per_residue(r, carry, br=br, dil=dil, n_blocks=n_blocks):
            unit(br, dil, r, r, QB, bias_diag)

            def per_block(n, carry2):
                k_start = r + dil * (n - 1) * QB
                unit(br, dil, k_start + dil * QB, k_start, 2 * QB, bias_band)
                return carry2

            if n_blocks > 1:
                lax.fori_loop(1, n_blocks, per_block, 0)
            return carry

        if dil == 1:
            per_residue(0, 0)
        else:
            lax.fori_loop(0, dil, per_residue, 0)

    def combine(i, carry):
        rows = pl.ds(pl.multiple_of(i * 256, 256), 256)
        l0, l1, l2 = lse_ref[0, rows, :], lse_ref[1, rows, :], lse_ref[2, rows, :]
        mx = jnp.maximum(jnp.maximum(l0, l1), l2)
        e0, e1, e2 = jnp.exp(l0 - mx), jnp.exp(l1 - mx), jnp.exp(l2 - mx)
        tot = e0 + e1 + e2
        out = (e0 * acc_ref[0, rows, :] + e1 * acc_ref[1, rows, :] + e2 * acc_ref[2, rows, :]) / tot
        o_ref[rows, :] = out.astype(o_ref.dtype)
        return carry

    lax.fori_loop(0, seq // 256, combine, 0)


def _att_call(qa, ka, va, batch, seq):
    blk = pl.BlockSpec((seq, LANES), lambda b, p: (b, p))
    n_br = len(DILATIONS)
    return pl.pallas_call(
        functools.partial(_att_kernel, seq=seq),
        grid=(batch, H_ATT // HEADS_PER_TILE),
        in_specs=[blk] * 3,
        out_specs=blk,
        out_shape=jax.ShapeDtypeStruct((batch * seq, ATT_WIDTH), BF16),
        scratch_shapes=[pltpu.VMEM((n_br, seq, LANES), F32), pltpu.VMEM((n_br, seq, LANES), F32)],
        compiler_params=pltpu.CompilerParams(dimension_semantics=("parallel", "parallel"),
                                             vmem_limit_bytes=VMEM_LIMIT),
    )(qa, ka, va)


def _post_kernel(*refs, tm, tiles_per_seq, shift):
    (x_ref, r_ref, a_ref, p_ref, lng_ref, lnb_ref, wo_ref, g1_ref, b1_ref, wup_ref, cw_ref, cb_ref,
     wdn_ref, wpg_ref, wpp_ref, g2_ref, b2_ref) = refs[:17]
    if shift:
        y_ref, cs_ref, h1_s, h1b_s, act_s, ua_s, ug_s, halo_s = refs[17:]
    else:
        pre0_ref, pre1_ref, y_ref, cs_ref, h1_s, h1b_s, act_s = refs[17:]

    h = _layer_norm(x_ref[...], lng_ref[...], lnb_ref[...])
    mix = (jnp.dot(r_ref[...].astype(BF16), wo_ref[:RET_WIDTH, :], preferred_element_type=F32)
           + jnp.dot(a_ref[...].astype(BF16), wo_ref[RET_WIDTH:, :], preferred_element_type=F32))
    h1 = _layer_norm(ALPHA * h + mix, g1_ref[...], b1_ref[...])
    h1_s[...] = h1
    h1b_s[...] = h1.astype(BF16)

    if shift:
        @pl.when(pl.program_id(0) % tiles_per_seq == 0)
        def _():
            halo_s[...] = jnp.zeros_like(halo_s)

    for j in range(D_FF // FF_CHUNK):
        conv = []
        for part, buf in ((0, ua_s if shift else None), (1, ug_s if shift else None)):
            c0 = part * D_FF + j * FF_CHUNK
            cols = slice(c0, c0 + FF_CHUNK)
            u = jnp.dot(h1b_s[...], wup_ref[:, cols], preferred_element_type=F32)
            if shift:
                buf[0:8, :] = halo_s[:, cols]
                buf[8:8 + tm, :] = u
                halo_s[:, cols] = u[tm - 8:, :]
                cs_ref[0, :, cols] = u[tm - (CONV_W - 1):, :]
                prev2 = buf[6:6 + tm, :]
                prev1 = buf[7:7 + tm, :]
            else:
                cs_ref[:, cols] = u
                prev2 = pre0_ref[:, cols]
                prev1 = pre1_ref[:, cols]
            conv.append(cb_ref[:, cols] + cw_ref[0:1, cols] * prev2 + cw_ref[1:2, cols] * prev1
                        + cw_ref[2:3, cols] * u)
        act_s[:, j * FF_CHUNK:(j + 1) * FF_CHUNK] = (conv[0] * _gelu_tanh(conv[1])).astype(BF16)

    f = jnp.dot(act_s[...], wdn_ref[...], preferred_element_type=F32)
    gate = _sigmoid(jnp.dot(h1b_s[...], wpg_ref[...], preferred_element_type=F32))
    e = gate * jnp.dot(p_ref[...].astype(BF16), wpp_ref[...], preferred_element_type=F32)
    y_ref[...] = _layer_norm(ALPHA * h1_s[...] + f + e, g2_ref[...], b2_ref[...])


def _post_call(x2d, r2d, a2d, p2d, weights, tm, tiles_per_seq, prefix=None):
    n = x2d.shape[0]
    shift = prefix is None
    two_f = 2 * D_FF
    row = lambda i: (i, 0)
    in_specs = [pl.BlockSpec((tm, D_MODEL), row), pl.BlockSpec((tm, RET_WIDTH), row),
                pl.BlockSpec((tm, ATT_WIDTH), row), pl.BlockSpec((tm, PLE_DIM), row)]
    in_specs += [_const_spec(w.shape) for w in weights]
    args = [x2d, r2d, a2d, p2d, *weights]
    scratch = [pltpu.VMEM((tm, D_MODEL), F32), pltpu.VMEM((tm, D_MODEL), BF16), pltpu.VMEM((tm, D_FF), BF16)]
    if shift:
        out_specs = [pl.BlockSpec((tm, D_MODEL), row),
                     pl.BlockSpec((1, CONV_W - 1, two_f), lambda i: (i // tiles_per_seq, 0, 0))]
        out_shape = [jax.ShapeDtypeStruct((n, D_MODEL), F32),
                     jax.ShapeDtypeStruct((n // (tm * tiles_per_seq), CONV_W - 1, two_f), F32)]
        scratch += [pltpu.VMEM((8 + tm, FF_CHUNK), F32), pltpu.VMEM((8 + tm, FF_CHUNK), F32),
                    pltpu.VMEM((8, two_f), F32)]
    else:
        in_specs += [pl.BlockSpec((tm, two_f), row)] * 2
        args += list(prefix)
        out_specs = [pl.BlockSpec((tm, D_MODEL), row), pl.BlockSpec((tm, two_f), row)]
        out_shape = [jax.ShapeDtypeStruct((n, D_MODEL), F32), jax.ShapeDtypeStruct((n, two_f), F32)]
    return pl.pallas_call(
        functools.partial(_post_kernel, tm=tm, tiles_per_seq=tiles_per_seq, shift=shift),
        grid=(n // tm,),
        in_specs=in_specs,
        out_specs=out_specs,
        out_shape=out_shape,
        scratch_shapes=scratch,
        compiler_params=pltpu.CompilerParams(dimension_semantics=("arbitrary",), vmem_limit_bytes=VMEM_LIMIT),
    )(*args)


def _sample_mix_kernel(qr_ref, kr_ref, vr_ref, gr_ref, st_ref, qa_ref, ka_ref, va_ref,
                       k1_ref, k4_ref, k16_ref, v1_ref, v4_ref, v16_ref,
                       r_ref, att_ref, nst_ref):
    hrow = lax.broadcasted_iota(jnp.int32, (H_RET, RET_WIDTH), 0)
    hlane = lax.broadcasted_iota(jnp.int32, (H_RET, RET_WIDTH), 1) // DK_RET
    own = hrow == hlane

    qm = jnp.where(own, qr_ref[0], 0.0)
    km = jnp.where(own, kr_ref[0], 0.0)
    v8 = vr_ref[0]
    g8 = gr_ref[0]
    st = st_ref[0]
    lg8 = _select_log_gamma(lax.broadcasted_iota(jnp.int32, (H_RET, 1), 0))
    cross = jnp.dot(qm.astype(BF16), st.astype(BF16), preferred_element_type=F32) * jnp.exp(lg8)
    qk = jnp.sum(qm * km, axis=-1, keepdims=True)
    o = qk * v8 + cross
    lg_rows = _select_log_gamma(lax.broadcasted_iota(jnp.int32, (H_RET * DK_RET, 1), 0) // DK_RET)
    outer = lax.dot_general(km, v8, (((0,), (0,)), ((), ())), preferred_element_type=F32,
                            precision=lax.Precision.HIGHEST)
    nst_ref[0] = jnp.exp(lg_rows) * st + outer
    mu = jnp.mean(o, axis=-1, keepdims=True)
    xc = o - mu
    var = jnp.mean(xc * xc, axis=-1, keepdims=True)
    r_ref[0] = xc * lax.rsqrt(var + LN_EPS) * (g8 * _sigmoid(g8))

    qam = jnp.where(own, qa_ref[0], 0.0) * ATT_SCALE
    k_new = ka_ref[0]
    v_new = va_ref[0]
    s_new = jnp.sum(qam * k_new, axis=-1, keepdims=True)
    outs, lses = [], []
    for kc_ref, vc_ref in ((k1_ref, v1_ref), (k4_ref, v4_ref), (k16_ref, v16_ref)):
        s = lax.dot_general(qam.astype(BF16), kc_ref[0].astype(BF16), (((1,), (1,)), ((), ())),
                            preferred_element_type=F32)
        m = jnp.maximum(jnp.max(s, axis=-1, keepdims=True), s_new)
        p = jnp.exp(s - m)
        p_new = jnp.exp(s_new - m)
        den = jnp.sum(p, axis=-1, keepdims=True) + p_new
        pv = jnp.dot(p.astype(BF16), vc_ref[0].astype(BF16), preferred_element_type=F32)
        outs.append((pv + p_new * v_new) / den)
        lses.append(m + jnp.log(den))
    mx = jnp.maximum(jnp.maximum(lses[0], lses[1]), lses[2])
    es = [jnp.exp(l - mx) for l in lses]
    comb = (es[0] * outs[0] + es[1] * outs[1] + es[2] * outs[2]) / (es[0] + es[1] + es[2])
    att_ref[0] = jnp.sum(jnp.where(own, comb, 0.0), axis=0, keepdims=True)


def _sample_mix_call(qr, kr, vr, gr, state, qa, ka, va, cache_k, cache_v):
    nb = qr.shape[0]
    n_past = cache_k.shape[1]
    row3 = pl.BlockSpec((1, 1, RET_WIDTH), lambda b: (b, 0, 0))
    head3 = pl.BlockSpec((1, H_RET, DK_RET), lambda b: (b, 0, 0))
    st_spec = pl.BlockSpec((1, H_RET * DK_RET, DK_RET), lambda b: (b, 0, 0))
    cache_args, cache_specs = [], []
    for cache in (cache_k, cache_v):
        for window, dil in DILATIONS:
            span = window // dil
            assert span == Q_BLOCK and n_past % (dil * span) == 0 and span * dil <= n_past
            cache_args.append(cache.reshape(nb, n_past // dil, dil * ATT_WIDTH))
            last = n_past // dil // span - 1
            cache_specs.append(pl.BlockSpec((1, span, ATT_WIDTH), lambda b, last=last: (b, last, 0)))
    r3 = lambda a: a.reshape(nb, 1, RET_WIDTH)
    h3 = lambda a: a.reshape(nb, H_RET, DK_RET)
    return pl.pallas_call(
        _sample_mix_kernel,
        grid=(nb,),
        in_specs=[row3, row3, head3, head3, st_spec, row3, row3, row3] + cache_specs,
        out_specs=[head3, row3, st_spec],
        out_shape=[jax.ShapeDtypeStruct((nb, H_RET, DK_RET), F32),
                   jax.ShapeDtypeStruct((nb, 1, ATT_WIDTH), F32),
                   jax.ShapeDtypeStruct((nb, H_RET * DK_RET, DK_RET), F32)],
        compiler_params=pltpu.CompilerParams(dimension_semantics=("parallel",), vmem_limit_bytes=VMEM_LIMIT),
    )(r3(qr), r3(kr), h3(vr), h3(gr), state.reshape(nb, H_RET * DK_RET, DK_RET), r3(qa), r3(ka), r3(va),
      *cache_args)


def kernel(x_prompt, x_sample, cache_k_win, cache_v_win, state_ret, state_conv, p_prompt, p_sample,
           ln_in_g, ln_in_b, w_in, w_out, ln1_g, ln1_b, w_up, conv_w, conv_b, w_down,
           w_ple_gate, w_ple_proj, ln2_g, ln2_b):
    B, S, _ = x_prompt.shape
    NB, T, _ = x_sample.shape
    assert T == 1 and w_in.shape[0] == DEPTH == 1 and S % PROJ_TILE == 0 and S % POST_TILE == 0
    two_f = 2 * D_FF
    vec = lambda a: a.reshape(1, -1)
    w_in_b = w_in[0].astype(BF16)
    post_w = (vec(ln_in_g), vec(ln_in_b), w_out[0].astype(BF16), vec(ln1_g[0]), vec(ln1_b[0]),
              w_up[0].astype(BF16), conv_w[0], vec(conv_b[0]), w_down[0].astype(BF16),
              w_ple_gate[0].astype(BF16), w_ple_proj[0].astype(BF16), vec(ln2_g[0]), vec(ln2_b[0]))

    pos_p = jnp.arange(S, dtype=jnp.int32)
    tabs_p = _rot_tables(pos_p, DK_RET, RET_ROPE_BASE) + _rot_tables(pos_p, ROT_DIMS, ATT_ROPE_THETA)
    xp = x_prompt.reshape(B * S, D_MODEL)
    qr, kr, vr, gr, qa, ka, va = _proj_call(xp, vec(ln_in_g), vec(ln_in_b), w_in_b, tabs_p, PROJ_TILE,
                                            S // PROJ_TILE, (BF16, F32, BF16, F32, F32, F32, F32))
    r_p, ret_fin = _ret_call(qr, kr, vr, gr, B, S)
    att_p = _att_call(qa, ka, va, B, S)
    y_p, conv_p = _post_call(xp, r_p, att_p, p_prompt[0].reshape(B * S, PLE_DIM), post_w,
                             POST_TILE, S // POST_TILE)
    keep = min(WINDOW_MAX, S)
    k_win_p = ka.reshape(B, S, H_ATT, HD_ATT)[:, S - keep:]
    v_win_p = va.reshape(B, S, H_ATT, HD_ATT)[:, S - keep:]

    pos_s = jnp.full((NB,), PAST_LEN, jnp.int32)
    tabs_s = _rot_tables(pos_s, DK_RET, RET_ROPE_BASE) + _rot_tables(pos_s, ROT_DIMS, ATT_ROPE_THETA)
    xs = x_sample.reshape(NB, D_MODEL)
    sqr, skr, svr, sgr, sqa, ska, sva = _proj_call(xs, vec(ln_in_g), vec(ln_in_b), w_in_b, tabs_s, NB, 1,
                                                   (F32,) * 7)
    n_past = cache_k_win.shape[2]
    r_s, att_s, nst = _sample_mix_call(sqr, skr, svr, sgr, state_ret[0], sqa, ska, sva,
                                       cache_k_win[0].reshape(NB, n_past, ATT_WIDTH),
                                       cache_v_win[0].reshape(NB, n_past, ATT_WIDTH))
    y_s, u_s = _post_call(xs, r_s.reshape(NB, RET_WIDTH), att_s.reshape(NB, ATT_WIDTH), p_sample[0].reshape(NB, PLE_DIM),
                          post_w, NB, 1, prefix=(state_conv[0][:, 0], state_conv[0][:, 1]))
    conv_s = jnp.stack([state_conv[0][:, 1], u_s], axis=1)

    return (y_p.reshape(B, S, D_MODEL), y_s.reshape(NB, 1, D_MODEL),
            k_win_p[None], v_win_p[None], ret_fin[None], conv_p[None],
            ska.reshape(1, NB, 1, H_ATT, HD_ATT), sva.reshape(1, NB, 1, H_ATT, HD_ATT),
            nst.reshape(1, NB, H_RET, DK_RET, DK_RET), conv_s[None])
```

```python
import functools
import math

import numpy as np
import jax
import jax.numpy as jnp
from jax import lax
from jax.experimental import pallas as pl
from jax.experimental.pallas import tpu as pltpu

F32 = jnp.float32
BF16 = jnp.bfloat16

D_MODEL = 1024
PAST_LEN = 16384
H_RET = 8
DK_RET = 64
RET_WIDTH = 512
RET_CHUNK = 128
RET_ROPE_BASE = 10000.0
H_ATT = 8
HD_ATT = 64
ATT_WIDTH = 512
ATT_ROPE_THETA = 500000.0
ROT_DIMS = HD_ATT // 4
DILATIONS = ((128, 1), (512, 4), (2048, 16))
WINDOW_MAX = 2048
Q_BLOCK = 128
PROJ_WIDTH = 4 * RET_WIDTH + 3 * ATT_WIDTH
D_FF = 2816
CONV_W = 3
PLE_DIM = 256
LN_EPS = 1e-5
DEPTH = 1
ALPHA = (2 * DEPTH) ** 0.25

LANES = 128
HEADS_PER_TILE = LANES // DK_RET
NEG_BIG = -1e30
LOG_GAMMA = tuple(math.log(1.0 - 2.0 ** (-5.0 - h)) for h in range(H_RET))
ATT_SCALE = HD_ATT ** -0.5
VMEM_LIMIT = 56 * 1024 * 1024

PROJ_TILE = 512
POST_TILE = 512
FF_CHUNK = 256


def _layer_norm(x, g, b):
    mu = jnp.mean(x, axis=-1, keepdims=True)
    xc = x - mu
    var = jnp.mean(xc * xc, axis=-1, keepdims=True)
    return xc * lax.rsqrt(var + LN_EPS) * g + b


def _sigmoid(x):
    return 1.0 / (1.0 + jnp.exp(-x))


def _gelu_tanh(x):
    return 0.5 * x * (1.0 + jnp.tanh(math.sqrt(2.0 / math.pi) * (x + 0.044715 * (x * x * x))))


def _select_log_gamma(head_idx):
    out = jnp.zeros(head_idx.shape, F32)
    for h in range(H_RET):
        out = jnp.where(head_idx == h, LOG_GAMMA[h], out)
    return out


def _const_spec(shape):
    nd = len(shape)
    return pl.BlockSpec(shape, lambda *_: (0,) * nd, pipeline_mode=pl.Buffered(1))


def _rotate(z, cos, s_lo, s_hi, half):
    up = pltpu.roll(z, LANES - half, 1)
    dn = pltpu.roll(z, half, 1)
    return z * cos + up * s_lo + dn * s_hi


def _proj_kernel(x_ref, g_ref, b_ref, w_ref, cr_ref, slr_ref, shr_ref, ca_ref, sla_ref, sha_ref,
                 qr_ref, kr_ref, vr_ref, gr_ref, qa_ref, ka_ref, va_ref):
    hb = _layer_norm(x_ref[...], g_ref[...], b_ref[...]).astype(BF16)
    outs = (qr_ref, kr_ref, vr_ref, gr_ref, qa_ref, ka_ref, va_ref)
    for grp, o_ref in enumerate(outs):
        for j in range(RET_WIDTH // LANES):
            c0 = grp * RET_WIDTH + j * LANES
            z = jnp.dot(hb, w_ref[:, c0:c0 + LANES], preferred_element_type=F32)
            if grp in (0, 1):
                z = _rotate(z, cr_ref[...], slr_ref[...], shr_ref[...], DK_RET // 2)
                if grp == 1:
                    z = z * (DK_RET ** -0.5)
            elif grp in (4, 5):
                z = _rotate(z, ca_ref[...], sla_ref[...], sha_ref[...], ROT_DIMS // 2)
            o_ref[:, j * LANES:(j + 1) * LANES] = z.astype(o_ref.dtype)


def _rot_tables(pos, n_rot, base):
    half = n_rot // 2
    inv = base ** (-jnp.arange(half, dtype=F32) / half)
    ang = pos.astype(F32)[:, None] * inv[None, :]
    cos, sin = jnp.cos(ang), jnp.sin(ang)
    l = np.arange(LANES) % DK_RET
    idx = np.where(l < half, l, np.where(l < n_rot, l - half, 0))
    lo = jnp.asarray(l < half)
    hi = jnp.asarray((l >= half) & (l < n_rot))
    cos_f = jnp.where(jnp.asarray(l < n_rot), cos[:, idx], 1.0)
    s_lo = jnp.where(lo, -sin[:, idx], 0.0)
    s_hi = jnp.where(hi, sin[:, idx], 0.0)
    return cos_f, s_lo, s_hi


def _proj_call(x2d, ln_g, ln_b, w_in_b, tabs, tm, tab_blocks, out_dtypes):
    n = x2d.shape[0]
    row = lambda i: (i, 0)
    tab = lambda i: (i % tab_blocks, 0)
    tab_spec = pl.BlockSpec((tm, LANES), tab)
    return pl.pallas_call(
        _proj_kernel,
        grid=(n // tm,),
        in_specs=[pl.BlockSpec((tm, D_MODEL), row), _const_spec((1, D_MODEL)), _const_spec((1, D_MODEL)),
                  _const_spec((D_MODEL, PROJ_WIDTH))] + [tab_spec] * 6,
        out_specs=[pl.BlockSpec((tm, RET_WIDTH), row)] * 7,
        out_shape=[jax.ShapeDtypeStruct((n, RET_WIDTH), dt) for dt in out_dtypes],
        compiler_params=pltpu.CompilerParams(dimension_semantics=("parallel",), vmem_limit_bytes=VMEM_LIMIT),
    )(x2d, ln_g, ln_b, w_in_b, *tabs)


RET_UNROLL = 4


def _ret_kernel(q_ref, k_ref, v_ref, g_ref, r_ref, st_ref, *, n_chunks):
    C = RET_CHUNK
    hp = pl.program_id(1)
    lane1 = lax.broadcasted_iota(jnp.int32, (1, LANES), 1)
    lg_lane = _select_log_gamma(HEADS_PER_TILE * hp + (lane1 >= DK_RET).astype(jnp.int32))

    ri = lax.broadcasted_iota(jnp.int32, (2 * C, C), 0)
    ci = lax.broadcasted_iota(jnp.int32, (2 * C, C), 1)
    rel = (jnp.where(ri >= C, ri - C, ri) - ci).astype(F32)
    lg_rows = _select_log_gamma(HEADS_PER_TILE * hp + (ri >= C).astype(jnp.int32))
    decay = jnp.where(rel >= 0, jnp.exp(lg_rows * jnp.maximum(rel, 0.0)), 0.0)

    tok = lax.broadcasted_iota(jnp.int32, (C, LANES), 0).astype(F32)
    cross_dec = jnp.exp(lg_lane * (tok + 1.0))
    k_dec = jnp.exp(lg_lane * (C - 1.0 - tok))
    sr = lax.broadcasted_iota(jnp.int32, (LANES, LANES), 0)
    sc = lax.broadcasted_iota(jnp.int32, (LANES, LANES), 1)
    same_head = ((sr >= DK_RET) == (sc >= DK_RET)).astype(F32)
    state_dec = jnp.exp(_select_log_gamma(HEADS_PER_TILE * hp + (sr >= DK_RET).astype(jnp.int32)) * float(C))

    def body(c, state):
        first = lax.broadcasted_iota(jnp.int32, (1, LANES), 1) < DK_RET
        off = pl.multiple_of(c * C, C)
        q = q_ref[pl.ds(off, C), :]
        k = k_ref[pl.ds(off, C), :]
        v = v_ref[pl.ds(off, C), :]
        g = g_ref[pl.ds(off, C), :]
        zero = jnp.zeros_like(q)
        qs = jnp.concatenate([jnp.where(first, q, zero), jnp.where(first, zero, q)], axis=0)
        s = lax.dot_general(qs, k.astype(BF16), (((1,), (1,)), ((), ())), preferred_element_type=F32)
        pv = jnp.dot((s * decay).astype(BF16), v, preferred_element_type=F32)
        inner = jnp.where(first, pv[:C], pv[C:])
        cross = jnp.dot(q, state.astype(BF16), preferred_element_type=F32) * cross_dec
        o = inner + cross
        upd = lax.dot_general((k * k_dec).astype(BF16), v, (((0,), (0,)), ((), ())),
                              preferred_element_type=F32)
        new_state = state_dec * state + same_head * upd
        s_a = jnp.sum(jnp.where(first, o, 0.0), axis=-1, keepdims=True)
        s_b = jnp.sum(jnp.where(first, 0.0, o), axis=-1, keepdims=True)
        xc = o - jnp.where(first, s_a, s_b) * (1.0 / DK_RET)
        sq = xc * xc
        v_a = jnp.sum(jnp.where(first, sq, 0.0), axis=-1, keepdims=True)
        v_b = jnp.sum(jnp.where(first, 0.0, sq), axis=-1, keepdims=True)
        rn = xc * lax.rsqrt(jnp.where(first, v_a, v_b) * (1.0 / DK_RET) + LN_EPS)
        r_ref[pl.ds(off, C), :] = (rn * (g * _sigmoid(g))).astype(r_ref.dtype)
        return new_state

    state = lax.fori_loop(0, n_chunks, body, jnp.zeros((LANES, LANES), F32), unroll=RET_UNROLL)
    st_ref[0, 0] = state[:DK_RET, :DK_RET]
    st_ref[0, 1] = state[DK_RET:, DK_RET:]


def _ret_call(qr, kr, vr, gr, batch, seq):
    blk = pl.BlockSpec((seq, LANES), lambda b, p: (b, p))
    n_pairs = H_RET // HEADS_PER_TILE
    return pl.pallas_call(
        functools.partial(_ret_kernel, n_chunks=seq // RET_CHUNK),
        grid=(batch, n_pairs),
        in_specs=[blk] * 4,
        out_specs=[blk, pl.BlockSpec((1, HEADS_PER_TILE, DK_RET, DK_RET), lambda b, p: (b, p, 0, 0))],
        out_shape=[jax.ShapeDtypeStruct((batch * seq, RET_WIDTH), BF16),
                   jax.ShapeDtypeStruct((batch, H_RET, DK_RET, DK_RET), F32)],
        compiler_params=pltpu.CompilerParams(dimension_semantics=("parallel", "parallel"),
                                             vmem_limit_bytes=VMEM_LIMIT),
    )(qr, kr, vr, gr)


ATT_UNROLL = {1: 3, 4: 3, 16: 4}


def _att_kernel(q_ref, k_ref, v_ref, o_ref, num_ref, den_ref, max_ref, *, seq):
    QB = Q_BLOCK
    ri = lax.broadcasted_iota(jnp.int32, (2 * QB, 2 * QB), 0)
    ci = lax.broadcasted_iota(jnp.int32, (2 * QB, 2 * QB), 1)
    qi = jnp.where(ri >= QB, ri - QB, ri)
    valid_band = ((ci < QB) & (ci >= qi)) | ((ci >= QB) & (ci - QB <= qi))
    bias_band = jnp.where(valid_band, 0.0, NEG_BIG)
    rd = lax.broadcasted_iota(jnp.int32, (2 * QB, QB), 0)
    cd = lax.broadcasted_iota(jnp.int32, (2 * QB, QB), 1)
    bias_diag = jnp.where(cd <= jnp.where(rd >= QB, rd - QB, rd), 0.0, NEG_BIG)
    ones = jnp.ones((2 * QB, LANES), BF16)

    def unit(br, dil, q_start, k_start, n_keys, bias):
        first = lax.broadcasted_iota(jnp.int32, (1, LANES), 1) < HD_ATT
        rows_q = pl.ds(q_start, QB, stride=dil) if dil > 1 else pl.ds(q_start, QB)
        rows_k = pl.ds(k_start, n_keys, stride=dil) if dil > 1 else pl.ds(k_start, n_keys)
        q = q_ref[rows_q, :] * ATT_SCALE
        qs = jnp.concatenate([jnp.where(first, q, 0.0), jnp.where(first, 0.0, q)], axis=0).astype(BF16)
        kk = k_ref[rows_k, :].astype(BF16)
        vv = v_ref[rows_k, :].astype(BF16)
        s = lax.dot_general(qs, kk, (((1,), (1,)), ((), ())), preferred_element_type=F32)
        s = s + bias
        m = jnp.max(s, axis=-1, keepdims=True)
        p = jnp.exp(s - m).astype(BF16)
        pv = jnp.dot(p, jnp.concatenate([vv, ones[:n_keys]], axis=1), preferred_element_type=F32)
        mb = jnp.broadcast_to(m, (2 * QB, LANES))
        num_ref[br, rows_q, :] = jnp.where(first, pv[:QB, :LANES], pv[QB:, :LANES])
        den_ref[br, rows_q, :] = jnp.where(first, pv[:QB, LANES:], pv[QB:, LANES:])
        max_ref[br, rows_q, :] = jnp.where(first, mb[:QB], mb[QB:])

    for br, (window, dil) in enumerate(DILATIONS):
        assert window // dil == QB
        sub_len = seq // dil
        n_blocks = sub_len // QB

        def per_residue(r, carry, br=br, dil=dil, n_blocks=n_blocks):
            unit(br, dil, r, r, QB, bias_diag)

            def per_block(n, carry2):
                k_start = r + dil * (n - 1) * QB
                unit(br, dil, k_start + dil * QB, k_start, 2 * QB, bias_band)
                return carry2

            if n_blocks > 1:
                lax.fori_loop(1, n_blocks, per_block, 0, unroll=ATT_UNROLL[dil])
            return carry

        if dil == 1:
            per_residue(0, 0)
        elif n_blocks > 1:
            lax.fori_loop(0, dil, per_residue, 0)
        else:
            lax.fori_loop(0, dil, per_residue, 0, unroll=ATT_UNROLL[dil])

    def combine(i, carry):
        rows = pl.ds(pl.multiple_of(i * 256, 256), 256)
        m0, m1, m2 = max_ref[0, rows, :], max_ref[1, rows, :], max_ref[2, rows, :]
        mx = jnp.maximum(jnp.maximum(m0, m1), m2)
        e0, e1, e2 = jnp.exp(m0 - mx), jnp.exp(m1 - mx), jnp.exp(m2 - mx)
        num = e0 * num_ref[0, rows, :] + e1 * num_ref[1, rows, :] + e2 * num_ref[2, rows, :]
        den = e0 * den_ref[0, rows, :] + e1 * den_ref[1, rows, :] + e2 * den_ref[2, rows, :]
        o_ref[rows, :] = (num / den).astype(o_ref.dtype)
        return carry

    lax.fori_loop(0, seq // 256, combine, 0)


def _att_call(qa, ka, va, batch, seq):
    blk = pl.BlockSpec((seq, LANES), lambda b, p: (b, p))
    n_br = len(DILATIONS)
    return pl.pallas_call(
        functools.partial(_att_kernel, seq=seq),
        grid=(batch, H_ATT // HEADS_PER_TILE),
        in_specs=[blk] * 3,
        out_specs=blk,
        out_shape=jax.ShapeDtypeStruct((batch * seq, ATT_WIDTH), BF16),
        scratch_shapes=[pltpu.VMEM((n_br, seq, LANES), F32)] * 3,
        compiler_params=pltpu.CompilerParams(dimension_semantics=("parallel", "parallel"),
                                             vmem_limit_bytes=VMEM_LIMIT),
    )(qa, ka, va)


def _post_kernel(*refs, tm, tiles_per_seq, shift):
    (x_ref, r_ref, a_ref, p_ref, lng_ref, lnb_ref, wo_ref, g1_ref, b1_ref, wup_ref, cw_ref, cb_ref,
     wdn_ref, wpg_ref, wpp_ref, g2_ref, b2_ref) = refs[:17]
    if shift:
        y_ref, cs_ref, h1_s, h1b_s, act_s, ua_s, ug_s, halo_s = refs[17:]
    else:
        pre0_ref, pre1_ref, y_ref, cs_ref, h1_s, h1b_s, act_s = refs[17:]

    h = _layer_norm(x_ref[...], lng_ref[...], lnb_ref[...])
    mix = (jnp.dot(r_ref[...].astype(BF16), wo_ref[:RET_WIDTH, :], preferred_element_type=F32)
           + jnp.dot(a_ref[...].astype(BF16), wo_ref[RET_WIDTH:, :], preferred_element_type=F32))
    h1 = _layer_norm(ALPHA * h + mix, g1_ref[...], b1_ref[...])
    h1_s[...] = h1
    h1b_s[...] = h1.astype(BF16)

    if shift:
        @pl.when(pl.program_id(0) % tiles_per_seq == 0)
        def _():
            halo_s[...] = jnp.zeros_like(halo_s)

    for j in range(D_FF // FF_CHUNK):
        conv = []
        for part, buf in ((0, ua_s if shift else None), (1, ug_s if shift else None)):
            c0 = part * D_FF + j * FF_CHUNK
            cols = slice(c0, c0 + FF_CHUNK)
            u = jnp.dot(h1b_s[...], wup_ref[:, cols], preferred_element_type=F32)
            if shift:
                buf[0:8, :] = halo_s[:, cols]
                buf[8:8 + tm, :] = u
                halo_s[:, cols] = u[tm - 8:, :]
                cs_ref[0, :, cols] = u[tm - (CONV_W - 1):, :]
                prev2 = buf[6:6 + tm, :]
                prev1 = buf[7:7 + tm, :]
            else:
                cs_ref[:, cols] = u
                prev2 = pre0_ref[:, cols]
                prev1 = pre1_ref[:, cols]
            conv.append(cb_ref[:, cols] + cw_ref[0:1, cols] * prev2 + cw_ref[1:2, cols] * prev1
                        + cw_ref[2:3, cols] * u)
        act_s[:, j * FF_CHUNK:(j + 1) * FF_CHUNK] = (conv[0] * _gelu_tanh(conv[1])).astype(BF16)

    f = jnp.dot(act_s[...], wdn_ref[...], preferred_element_type=F32)
    gate = _sigmoid(jnp.dot(h1b_s[...], wpg_ref[...], preferred_element_type=F32))
    e = gate * jnp.dot(p_ref[...].astype(BF16), wpp_ref[...], preferred_element_type=F32)
    y_ref[...] = _layer_norm(ALPHA * h1_s[...] + f + e, g2_ref[...], b2_ref[...])


def _post_call(x2d, r2d, a2d, p2d, weights, tm, tiles_per_seq, prefix=None):
    n = x2d.shape[0]
    shift = prefix is None
    two_f = 2 * D_FF
    row = lambda i: (i, 0)
    in_specs = [pl.BlockSpec((tm, D_MODEL), row), pl.BlockSpec((tm, RET_WIDTH), row),
                pl.BlockSpec((tm, ATT_WIDTH), row), pl.BlockSpec((tm, PLE_DIM), row)]
    in_specs += [_const_spec(w.shape) for w in weights]
    args = [x2d, r2d, a2d, p2d, *weights]
    scratch = [pltpu.VMEM((tm, D_MODEL), F32), pltpu.VMEM((tm, D_MODEL), BF16), pltpu.VMEM((tm, D_FF), BF16)]
    if shift:
        out_specs = [pl.BlockSpec((tm, D_MODEL), row),
                     pl.BlockSpec((1, CONV_W - 1, two_f), lambda i: (i // tiles_per_seq, 0, 0))]
        out_shape = [jax.ShapeDtypeStruct((n, D_MODEL), F32),
                     jax.ShapeDtypeStruct((n // (tm * tiles_per_seq), CONV_W - 1, two_f), F32)]
        scratch += [pltpu.VMEM((8 + tm, FF_CHUNK), F32), pltpu.VMEM((8 + tm, FF_CHUNK), F32),
                    pltpu.VMEM((8, two_f), F32)]
    else:
        in_specs += [pl.BlockSpec((tm, two_f), row)] * 2
        args += list(prefix)
        out_specs = [pl.BlockSpec((tm, D_MODEL), row), pl.BlockSpec((tm, two_f), row)]
        out_shape = [jax.ShapeDtypeStruct((n, D_MODEL), F32), jax.ShapeDtypeStruct((n, two_f), F32)]
    return pl.pallas_call(
        functools.partial(_post_kernel, tm=tm, tiles_per_seq=tiles_per_seq, shift=shift),
        grid=(n // tm,),
        in_specs=in_specs,
        out_specs=out_specs,
        out_shape=out_shape,
        scratch_shapes=scratch,
        compiler_params=pltpu.CompilerParams(dimension_semantics=("arbitrary",), vmem_limit_bytes=VMEM_LIMIT),
    )(*args)


def _sample_mix_kernel(qr_ref, kr_ref, vr_ref, gr_ref, st_ref, qa_ref, ka_ref, va_ref,
                       k1_ref, k4_ref, k16_ref, v1_ref, v4_ref, v16_ref,
                       r_ref, att_ref, nst_ref):
    hrow = lax.broadcasted_iota(jnp.int32, (H_RET, RET_WIDTH), 0)
    hlane = lax.broadcasted_iota(jnp.int32, (H_RET, RET_WIDTH), 1) // DK_RET
    own = hrow == hlane

    qm = jnp.where(own, qr_ref[0], 0.0)
    km = jnp.where(own, kr_ref[0], 0.0)
    v8 = vr_ref[0]
    g8 = gr_ref[0]
    st = st_ref[0]
    lg8 = _select_log_gamma(lax.broadcasted_iota(jnp.int32, (H_RET, 1), 0))
    cross = jnp.dot(qm.astype(BF16), st.astype(BF16), preferred_element_type=F32) * jnp.exp(lg8)
    qk = jnp.sum(qm * km, axis=-1, keepdims=True)
    o = qk * v8 + cross
    lg_rows = _select_log_gamma(lax.broadcasted_iota(jnp.int32, (H_RET * DK_RET, 1), 0) // DK_RET)
    outer = lax.dot_general(km, v8, (((0,), (0,)), ((), ())), preferred_element_type=F32,
                            precision=lax.Precision.HIGHEST)
    nst_ref[0] = jnp.exp(lg_rows) * st + outer
    mu = jnp.mean(o, axis=-1, keepdims=True)
    xc = o - mu
    var = jnp.mean(xc * xc, axis=-1, keepdims=True)
    r_ref[0] = xc * lax.rsqrt(var + LN_EPS) * (g8 * _sigmoid(g8))

    q8 = qa_ref[0] * ATT_SCALE
    k_new = ka_ref[0]
    v_new = va_ref[0]
    s_new = jnp.sum(q8 * k_new, axis=-1, keepdims=True)
    n_rows = k1_ref.shape[0] * H_ATT
    key_head = lax.broadcasted_iota(jnp.int32, (H_ATT, n_rows), 1) % H_ATT
    bias = jnp.where(key_head == lax.broadcasted_iota(jnp.int32, (H_ATT, n_rows), 0), 0.0, NEG_BIG)
    outs, lses = [], []
    for kc_ref, vc_ref in ((k1_ref, v1_ref), (k4_ref, v4_ref), (k16_ref, v16_ref)):
        k2 = kc_ref[...].reshape(n_rows, HD_ATT).astype(BF16)
        v2 = vc_ref[...].reshape(n_rows, HD_ATT).astype(BF16)
        s = lax.dot_general(q8.astype(BF16), k2, (((1,), (1,)), ((), ())), preferred_element_type=F32) + bias
        m = jnp.maximum(jnp.max(s, axis=-1, keepdims=True), s_new)
        p = jnp.exp(s - m)
        p_new = jnp.exp(s_new - m)
        den = jnp.sum(p, axis=-1, keepdims=True) + p_new
        pv = jnp.dot(p.astype(BF16), v2, preferred_element_type=F32)
        outs.append((pv + p_new * v_new) / den)
        lses.append(m + jnp.log(den))
    mx = jnp.maximum(jnp.maximum(lses[0], lses[1]), lses[2])
    es = [jnp.exp(l - mx) for l in lses]
    att_ref[0] = (es[0] * outs[0] + es[1] * outs[1] + es[2] * outs[2]) / (es[0] + es[1] + es[2])


def _sample_mix_call(qr, kr, vr, gr, state, qa, ka, va, cache_k, cache_v):
    nb = qr.shape[0]
    n_past = cache_k.shape[1]
    row3 = pl.BlockSpec((1, 1, RET_WIDTH), lambda b: (b, 0, 0))
    head3 = pl.BlockSpec((1, H_RET, DK_RET), lambda b: (b, 0, 0))
    st_spec = pl.BlockSpec((1, H_RET * DK_RET, DK_RET), lambda b: (b, 0, 0))
    cache_args, cache_specs = [], []
    for cache in (cache_k, cache_v):
        for window, dil in DILATIONS:
            span = window // dil
            assert span == Q_BLOCK and n_past % (dil * span) == 0 and span * dil <= n_past
            cache_args.append(cache.reshape(nb, n_past // dil, dil, H_ATT, HD_ATT))
            last = n_past // dil // span - 1
            cache_specs.append(pl.BlockSpec((None, span, None, H_ATT, HD_ATT),
                                            lambda b, last=last: (b, last, 0, 0, 0)))
    r3 = lambda a: a.reshape(nb, 1, RET_WIDTH)
    h3 = lambda a: a.reshape(nb, H_RET, DK_RET)
    return pl.pallas_call(
        _sample_mix_kernel,
        grid=(nb,),
        in_specs=[row3, row3, head3, head3, st_spec, head3, head3, head3] + cache_specs,
        out_specs=[head3, head3, st_spec],
        out_shape=[jax.ShapeDtypeStruct((nb, H_RET, DK_RET), F32),
                   jax.ShapeDtypeStruct((nb, H_ATT, HD_ATT), F32),
                   jax.ShapeDtypeStruct((nb, H_RET * DK_RET, DK_RET), F32)],
        compiler_params=pltpu.CompilerParams(dimension_semantics=("parallel",), vmem_limit_bytes=VMEM_LIMIT),
    )(r3(qr), r3(kr), h3(vr), h3(gr), state.reshape(nb, H_RET * DK_RET, DK_RET), h3(qa), h3(ka), h3(va),
      *cache_args)


def kernel(x_prompt, x_sample, cache_k_win, cache_v_win, state_ret, state_conv, p_prompt, p_sample,
           ln_in_g, ln_in_b, w_in, w_out, ln1_g, ln1_b, w_up, conv_w, conv_b, w_down,
           w_ple_gate, w_ple_proj, ln2_g, ln2_b):
    B, S, _ = x_prompt.shape
    NB, T, _ = x_sample.shape
    assert T == 1 and w_in.shape[0] == DEPTH == 1 and S % PROJ_TILE == 0 and S % POST_TILE == 0
    two_f = 2 * D_FF
    vec = lambda a: a.reshape(1, -1)
    w_in_b = w_in[0].astype(BF16)
    post_w = (vec(ln_in_g), vec(ln_in_b), w_out[0].astype(BF16), vec(ln1_g[0]), vec(ln1_b[0]),
              w_up[0].astype(BF16), conv_w[0], vec(conv_b[0]), w_down[0].astype(BF16),
              w_ple_gate[0].astype(BF16), w_ple_proj[0].astype(BF16), vec(ln2_g[0]), vec(ln2_b[0]))

    pos_p = jnp.arange(S, dtype=jnp.int32)
    tabs_p = _rot_tables(pos_p, DK_RET, RET_ROPE_BASE) + _rot_tables(pos_p, ROT_DIMS, ATT_ROPE_THETA)
    xp = x_prompt.reshape(B * S, D_MODEL)
    qr, kr, vr, gr, qa, ka, va = _proj_call(xp, vec(ln_in_g), vec(ln_in_b), w_in_b, tabs_p, PROJ_TILE,
                                            S // PROJ_TILE, (BF16, F32, BF16, F32, F32, F32, F32))
    r_p, ret_fin = _ret_call(qr, kr, vr, gr, B, S)
    att_p = _att_call(qa, ka, va, B, S)
    y_p, conv_p = _post_call(xp, r_p, att_p, p_prompt[0].reshape(B * S, PLE_DIM), post_w,
                             POST_TILE, S // POST_TILE)
    keep = min(WINDOW_MAX, S)
    k_win_p = ka.reshape(B, S, H_ATT, HD_ATT)[:, S - keep:]
    v_win_p = va.reshape(B, S, H_ATT, HD_ATT)[:, S - keep:]

    pos_s = jnp.full((NB,), PAST_LEN, jnp.int32)
    tabs_s = _rot_tables(pos_s, DK_RET, RET_ROPE_BASE) + _rot_tables(pos_s, ROT_DIMS, ATT_ROPE_THETA)
    xs = x_sample.reshape(NB, D_MODEL)
    sqr, skr, svr, sgr, sqa, ska, sva = _proj_call(xs, vec(ln_in_g), vec(ln_in_b), w_in_b, tabs_s, NB, 1,
                                                   (F32,) * 7)
    r_s, att_s, nst = _sample_mix_call(sqr, skr, svr, sgr, state_ret[0], sqa, ska, sva,
                                       cache_k_win[0], cache_v_win[0])
    y_s, u_s = _post_call(xs, r_s.reshape(NB, RET_WIDTH), att_s.reshape(NB, ATT_WIDTH), p_sample[0].reshape(NB, PLE_DIM),
                          post_w, NB, 1, prefix=(state_conv[0][:, 0], state_conv[0][:, 1]))
    conv_s = jnp.stack([state_conv[0][:, 1], u_s], axis=1)

    return (y_p.reshape(B, S, D_MODEL), y_s.reshape(NB, 1, D_MODEL),
            k_win_p[None], v_win_p[None], ret_fin[None], conv_p[None],
            ska.reshape(1, NB, 1, H_ATT, HD_ATT), sva.reshape(1, NB, 1, H_ATT, HD_ATT),
            nst.reshape(1, NB, H_RET, DK_RET, DK_RET), conv_s[None])
```

```python
import functools
import math

import numpy as np
import jax
import jax.numpy as jnp
from jax import lax
from jax.experimental import pallas as pl
from jax.experimental.pallas import tpu as pltpu

F32 = jnp.float32
BF16 = jnp.bfloat16

D_MODEL = 1024
PAST_LEN = 16384
H_RET = 8
DK_RET = 64
RET_WIDTH = 512
RET_CHUNK = 128
RET_ROPE_BASE = 10000.0
H_ATT = 8
HD_ATT = 64
ATT_WIDTH = 512
ATT_ROPE_THETA = 500000.0
ROT_DIMS = HD_ATT // 4
DILATIONS = ((128, 1), (512, 4), (2048, 16))
WINDOW_MAX = 2048
Q_BLOCK = 128
PROJ_WIDTH = 4 * RET_WIDTH + 3 * ATT_WIDTH
D_FF = 2816
CONV_W = 3
PLE_DIM = 256
LN_EPS = 1e-5
DEPTH = 1
ALPHA = (2 * DEPTH) ** 0.25

LANES = 128
HEADS_PER_TILE = LANES // DK_RET
NEG_BIG = -1e30
LOG_GAMMA = tuple(math.log(1.0 - 2.0 ** (-5.0 - h)) for h in range(H_RET))
ATT_SCALE = HD_ATT ** -0.5
VMEM_LIMIT = 56 * 1024 * 1024

PROJ_TILE = 512
POST_TILE = 512
FF_CHUNK = 256


def _layer_norm(x, g, b):
    mu = jnp.mean(x, axis=-1, keepdims=True)
    xc = x - mu
    var = jnp.mean(xc * xc, axis=-1, keepdims=True)
    return xc * lax.rsqrt(var + LN_EPS) * g + b


def _sigmoid(x):
    return 1.0 / (1.0 + jnp.exp(-x))


def _gelu_tanh(x):
    return 0.5 * x * (1.0 + jnp.tanh(math.sqrt(2.0 / math.pi) * (x + 0.044715 * (x * x * x))))


def _select_log_gamma(head_idx):
    out = jnp.zeros(head_idx.shape, F32)
    for h in range(H_RET):
        out = jnp.where(head_idx == h, LOG_GAMMA[h], out)
    return out


def _const_spec(shape):
    nd = len(shape)
    return pl.BlockSpec(shape, lambda *_: (0,) * nd, pipeline_mode=pl.Buffered(1))


def _rotate(z, cos, s_lo, s_hi, half):
    up = pltpu.roll(z, LANES - half, 1)
    dn = pltpu.roll(z, half, 1)
    return z * cos + up * s_lo + dn * s_hi


def _proj_kernel(x_ref, g_ref, b_ref, w_ref, cr_ref, slr_ref, shr_ref, ca_ref, sla_ref, sha_ref,
                 qr_ref, kr_ref, vr_ref, gr_ref, qa_ref, ka_ref, va_ref, *t_refs):
    hb = _layer_norm(x_ref[...], g_ref[...], b_ref[...]).astype(BF16)
    outs = (qr_ref, kr_ref, vr_ref, gr_ref, qa_ref, ka_ref, va_ref)
    for grp, o_ref in enumerate(outs):
        zg = jnp.dot(hb, w_ref[:, grp * RET_WIDTH:(grp + 1) * RET_WIDTH], preferred_element_type=F32)
        for j in range(RET_WIDTH // LANES):
            z = zg[:, j * LANES:(j + 1) * LANES]
            if grp in (0, 1):
                z = _rotate(z, cr_ref[...], slr_ref[...], shr_ref[...], DK_RET // 2)
                if grp == 1:
                    z = z * (DK_RET ** -0.5)
            elif grp in (4, 5):
                z = _rotate(z, ca_ref[...], sla_ref[...], sha_ref[...], ROT_DIMS // 2)
            o_ref[:, j * LANES:(j + 1) * LANES] = z.astype(o_ref.dtype)
            if t_refs and grp in (5, 6):
                zt = z.T
                for hh in range(HEADS_PER_TILE):
                    t_refs[grp - 5][0, HEADS_PER_TILE * j + hh] = zt[hh * HD_ATT:(hh + 1) * HD_ATT]


def _rot_tables(pos, n_rot, base):
    half = n_rot // 2
    inv = base ** (-jnp.arange(half, dtype=F32) / half)
    ang = pos.astype(F32)[:, None] * inv[None, :]
    cos, sin = jnp.cos(ang), jnp.sin(ang)
    l = np.arange(LANES) % DK_RET
    idx = np.where(l < half, l, np.where(l < n_rot, l - half, 0))
    lo = jnp.asarray(l < half)
    hi = jnp.asarray((l >= half) & (l < n_rot))
    cos_f = jnp.where(jnp.asarray(l < n_rot), cos[:, idx], 1.0)
    s_lo = jnp.where(lo, -sin[:, idx], 0.0)
    s_hi = jnp.where(hi, sin[:, idx], 0.0)
    return cos_f, s_lo, s_hi


def _proj_call(x2d, ln_g, ln_b, w_in_b, tabs, tm, tab_blocks, out_dtypes, seq_for_transposed=None):
    n = x2d.shape[0]
    row = lambda i: (i, 0)
    tab = lambda i: (i % tab_blocks, 0)
    tab_spec = pl.BlockSpec((tm, LANES), tab)
    out_specs = [pl.BlockSpec((tm, RET_WIDTH), row)] * 7
    out_shape = [jax.ShapeDtypeStruct((n, RET_WIDTH), dt) for dt in out_dtypes]
    if seq_for_transposed is not None:
        tps = seq_for_transposed // tm
        out_specs += [pl.BlockSpec((1, H_ATT, HD_ATT, tm), lambda i: (i // tps, 0, 0, i % tps))] * 2
        out_shape += [jax.ShapeDtypeStruct((n // seq_for_transposed, H_ATT, HD_ATT, seq_for_transposed), F32)] * 2
    return pl.pallas_call(
        _proj_kernel,
        grid=(n // tm,),
        in_specs=[pl.BlockSpec((tm, D_MODEL), row), _const_spec((1, D_MODEL)), _const_spec((1, D_MODEL)),
                  _const_spec((D_MODEL, PROJ_WIDTH))] + [tab_spec] * 6,
        out_specs=out_specs,
        out_shape=out_shape,
        compiler_params=pltpu.CompilerParams(dimension_semantics=("parallel",), vmem_limit_bytes=VMEM_LIMIT),
    )(x2d, ln_g, ln_b, w_in_b, *tabs)


RET_UNROLL = 4


def _ret_kernel(q_ref, k_ref, v_ref, g_ref, r_ref, st_ref, *, n_chunks):
    C = RET_CHUNK
    hp = pl.program_id(1)
    lane1 = lax.broadcasted_iota(jnp.int32, (1, LANES), 1)
    lg_lane = _select_log_gamma(HEADS_PER_TILE * hp + (lane1 >= DK_RET).astype(jnp.int32))

    ri = lax.broadcasted_iota(jnp.int32, (2 * C, C), 0)
    ci = lax.broadcasted_iota(jnp.int32, (2 * C, C), 1)
    rel = (jnp.where(ri >= C, ri - C, ri) - ci).astype(F32)
    lg_rows = _select_log_gamma(HEADS_PER_TILE * hp + (ri >= C).astype(jnp.int32))
    decay = jnp.where(rel >= 0, jnp.exp(lg_rows * jnp.maximum(rel, 0.0)), 0.0)

    tok = lax.broadcasted_iota(jnp.int32, (C, LANES), 0).astype(F32)
    cross_dec = jnp.exp(lg_lane * (tok + 1.0))
    k_dec = jnp.exp(lg_lane * (C - 1.0 - tok))
    sr = lax.broadcasted_iota(jnp.int32, (LANES, LANES), 0)
    sc = lax.broadcasted_iota(jnp.int32, (LANES, LANES), 1)
    same_head = ((sr >= DK_RET) == (sc >= DK_RET)).astype(F32)
    state_dec = jnp.exp(_select_log_gamma(HEADS_PER_TILE * hp + (sr >= DK_RET).astype(jnp.int32)) * float(C))

    def body(c, state):
        first = lax.broadcasted_iota(jnp.int32, (1, LANES), 1) < DK_RET
        off = pl.multiple_of(c * C, C)
        q = q_ref[pl.ds(off, C), :]
        k = k_ref[pl.ds(off, C), :]
        v = v_ref[pl.ds(off, C), :]
        g = g_ref[pl.ds(off, C), :]
        zero = jnp.zeros_like(q)
        qs = jnp.concatenate([jnp.where(first, q, zero), jnp.where(first, zero, q)], axis=0)
        s = lax.dot_general(qs, k.astype(BF16), (((1,), (1,)), ((), ())), preferred_element_type=F32)
        pv = jnp.dot((s * decay).astype(BF16), v, preferred_element_type=F32)
        inner = jnp.where(first, pv[:C], pv[C:])
        cross = jnp.dot(q, state.astype(BF16), preferred_element_type=F32) * cross_dec
        o = inner + cross
        upd = lax.dot_general((k * k_dec).astype(BF16), v, (((0,), (0,)), ((), ())),
                              preferred_element_type=F32)
        new_state = state_dec * state + same_head * upd
        s_a = jnp.sum(jnp.where(first, o, 0.0), axis=-1, keepdims=True)
        s_b = jnp.sum(jnp.where(first, 0.0, o), axis=-1, keepdims=True)
        xc = o - jnp.where(first, s_a, s_b) * (1.0 / DK_RET)
        sq = xc * xc
        v_a = jnp.sum(jnp.where(first, sq, 0.0), axis=-1, keepdims=True)
        v_b = jnp.sum(jnp.where(first, 0.0, sq), axis=-1, keepdims=True)
        rn = xc * lax.rsqrt(jnp.where(first, v_a, v_b) * (1.0 / DK_RET) + LN_EPS)
        r_ref[pl.ds(off, C), :] = (rn * (g * _sigmoid(g))).astype(r_ref.dtype)
        return new_state

    state = lax.fori_loop(0, n_chunks, body, jnp.zeros((LANES, LANES), F32), unroll=RET_UNROLL)
    st_ref[0, 0] = state[:DK_RET, :DK_RET]
    st_ref[0, 1] = state[DK_RET:, DK_RET:]


def _ret_call(qr, kr, vr, gr, batch, seq):
    blk = pl.BlockSpec((seq, LANES), lambda b, p: (b, p))
    n_pairs = H_RET // HEADS_PER_TILE
    return pl.pallas_call(
        functools.partial(_ret_kernel, n_chunks=seq // RET_CHUNK),
        grid=(batch, n_pairs),
        in_specs=[blk] * 4,
        out_specs=[blk, pl.BlockSpec((1, HEADS_PER_TILE, DK_RET, DK_RET), lambda b, p: (b, p, 0, 0))],
        out_shape=[jax.ShapeDtypeStruct((batch * seq, RET_WIDTH), BF16),
                   jax.ShapeDtypeStruct((batch, H_RET, DK_RET, DK_RET), F32)],
        compiler_params=pltpu.CompilerParams(dimension_semantics=("parallel", "parallel"),
                                             vmem_limit_bytes=VMEM_LIMIT),
    )(qr, kr, vr, gr)


ATT_UNROLL = {1: 3, 4: 3, 16: 4}


def _att_kernel(q_ref, k_ref, v_ref, o_ref, num_ref, den_ref, max_ref, *, seq):
    QB = Q_BLOCK
    ri = lax.broadcasted_iota(jnp.int32, (2 * QB, 2 * QB), 0)
    ci = lax.broadcasted_iota(jnp.int32, (2 * QB, 2 * QB), 1)
    qi = jnp.where(ri >= QB, ri - QB, ri)
    valid_band = ((ci < QB) & (ci >= qi)) | ((ci >= QB) & (ci - QB <= qi))
    bias_band = jnp.where(valid_band, 0.0, NEG_BIG)
    rd = lax.broadcasted_iota(jnp.int32, (2 * QB, QB), 0)
    cd = lax.broadcasted_iota(jnp.int32, (2 * QB, QB), 1)
    bias_diag = jnp.where(cd <= jnp.where(rd >= QB, rd - QB, rd), 0.0, NEG_BIG)
    ones = jnp.ones((2 * QB, LANES), BF16)

    def unit(br, dil, q_start, k_start, n_keys, bias):
        first = lax.broadcasted_iota(jnp.int32, (1, LANES), 1) < HD_ATT
        rows_q = pl.ds(q_start, QB, stride=dil) if dil > 1 else pl.ds(q_start, QB)
        rows_k = pl.ds(k_start, n_keys, stride=dil) if dil > 1 else pl.ds(k_start, n_keys)
        q = q_ref[rows_q, :] * ATT_SCALE
        qs = jnp.concatenate([jnp.where(first, q, 0.0), jnp.where(first, 0.0, q)], axis=0).astype(BF16)
        kk = k_ref[rows_k, :].astype(BF16)
        vv = v_ref[rows_k, :].astype(BF16)
        s = lax.dot_general(qs, kk, (((1,), (1,)), ((), ())), preferred_element_type=F32)
        s = s + bias
        m = jnp.max(s, axis=-1, keepdims=True)
        p = jnp.exp(s - m).astype(BF16)
        pv = jnp.dot(p, jnp.concatenate([vv, ones[:n_keys]], axis=1), preferred_element_type=F32)
        mb = jnp.broadcast_to(m, (2 * QB, LANES))
        num_ref[br, rows_q, :] = jnp.where(first, pv[:QB, :LANES], pv[QB:, :LANES])
        den_ref[br, rows_q, :] = jnp.where(first, pv[:QB, LANES:], pv[QB:, LANES:])
        max_ref[br, rows_q, :] = jnp.where(first, mb[:QB], mb[QB:])

    for br, (window, dil) in enumerate(DILATIONS):
        assert window // dil == QB
        sub_len = seq // dil
        n_blocks = sub_len // QB

        def per_residue(r, carry, br=br, dil=dil, n_blocks=n_blocks):
            unit(br, dil, r, r, QB, bias_diag)

            def per_block(n, carry2):
                k_start = r + dil * (n - 1) * QB
                unit(br, dil, k_start + dil * QB, k_start, 2 * QB, bias_band)
                return carry2

            if n_blocks > 1:
                lax.fori_loop(1, n_blocks, per_block, 0, unroll=ATT_UNROLL[dil])
            return carry

        if dil == 1:
            per_residue(0, 0)
        elif n_blocks > 1:
            lax.fori_loop(0, dil, per_residue, 0)
        else:
            lax.fori_loop(0, dil, per_residue, 0, unroll=ATT_UNROLL[dil])

    def combine(i, carry):
        rows = pl.ds(pl.multiple_of(i * 256, 256), 256)
        m0, m1, m2 = max_ref[0, rows, :], max_ref[1, rows, :], max_ref[2, rows, :]
        mx = jnp.maximum(jnp.maximum(m0, m1), m2)
        e0, e1, e2 = jnp.exp(m0 - mx), jnp.exp(m1 - mx), jnp.exp(m2 - mx)
        num = e0 * num_ref[0, rows, :] + e1 * num_ref[1, rows, :] + e2 * num_ref[2, rows, :]
        den = e0 * den_ref[0, rows, :] + e1 * den_ref[1, rows, :] + e2 * den_ref[2, rows, :]
        o_ref[rows, :] = (num / den).astype(o_ref.dtype)
        return carry

    lax.fori_loop(0, seq // 256, combine, 0)


def _att_call(qa, ka, va, batch, seq):
    blk = pl.BlockSpec((seq, LANES), lambda b, p: (b, p))
    n_br = len(DILATIONS)
    return pl.pallas_call(
        functools.partial(_att_kernel, seq=seq),
        grid=(batch, H_ATT // HEADS_PER_TILE),
        in_specs=[blk] * 3,
        out_specs=blk,
        out_shape=jax.ShapeDtypeStruct((batch * seq, ATT_WIDTH), BF16),
        scratch_shapes=[pltpu.VMEM((n_br, seq, LANES), F32)] * 3,
        compiler_params=pltpu.CompilerParams(dimension_semantics=("parallel", "parallel"),
                                             vmem_limit_bytes=VMEM_LIMIT),
    )(qa, ka, va)


def _post_kernel(*refs, tm, tiles_per_seq, shift):
    (x_ref, r_ref, a_ref, p_ref, lng_ref, lnb_ref, wo_ref, g1_ref, b1_ref, wup_ref, cw_ref, cb_ref,
     wdn_ref, wpg_ref, wpp_ref, g2_ref, b2_ref) = refs[:17]
    if shift:
        y_ref, cs_ref, h1_s, h1b_s, act_s, ua_s, ug_s, halo_s = refs[17:]
    else:
        pre0_ref, pre1_ref, y_ref, cs_ref, h1_s, h1b_s, act_s = refs[17:]

    h = _layer_norm(x_ref[...], lng_ref[...], lnb_ref[...])
    mix = (jnp.dot(r_ref[...].astype(BF16), wo_ref[:RET_WIDTH, :], preferred_element_type=F32)
           + jnp.dot(a_ref[...].astype(BF16), wo_ref[RET_WIDTH:, :], preferred_element_type=F32))
    h1 = _layer_norm(ALPHA * h + mix, g1_ref[...], b1_ref[...])
    h1_s[...] = h1
    h1b_s[...] = h1.astype(BF16)

    if shift:
        @pl.when(pl.program_id(0) % tiles_per_seq == 0)
        def _():
            halo_s[...] = jnp.zeros_like(halo_s)

    for j in range(D_FF // FF_CHUNK):
        conv = []
        for part, buf in ((0, ua_s if shift else None), (1, ug_s if shift else None)):
            c0 = part * D_FF + j * FF_CHUNK
            cols = slice(c0, c0 + FF_CHUNK)
            u = jnp.dot(h1b_s[...], wup_ref[:, cols], preferred_element_type=F32)
            if shift:
                buf[0:8, :] = halo_s[:, cols]
                buf[8:8 + tm, :] = u
                halo_s[:, cols] = u[tm - 8:, :]
                cs_ref[0, :, cols] = u[tm - (CONV_W - 1):, :]
                prev2 = buf[6:6 + tm, :]
                prev1 = buf[7:7 + tm, :]
            else:
                cs_ref[:, cols] = u
                prev2 = pre0_ref[:, cols]
                prev1 = pre1_ref[:, cols]
            conv.append(cb_ref[:, cols] + cw_ref[0:1, cols] * prev2 + cw_ref[1:2, cols] * prev1
                        + cw_ref[2:3, cols] * u)
        act_s[:, j * FF_CHUNK:(j + 1) * FF_CHUNK] = (conv[0] * _gelu_tanh(conv[1])).astype(BF16)

    f = jnp.dot(act_s[...], wdn_ref[...], preferred_element_type=F32)
    gate = _sigmoid(jnp.dot(h1b_s[...], wpg_ref[...], preferred_element_type=F32))
    e = gate * jnp.dot(p_ref[...].astype(BF16), wpp_ref[...], preferred_element_type=F32)
    y_ref[...] = _layer_norm(ALPHA * h1_s[...] + f + e, g2_ref[...], b2_ref[...])


def _post_call(x2d, r2d, a2d, p2d, weights, tm, tiles_per_seq, prefix=None):
    n = x2d.shape[0]
    shift = prefix is None
    two_f = 2 * D_FF
    row = lambda i: (i, 0)
    in_specs = [pl.BlockSpec((tm, D_MODEL), row), pl.BlockSpec((tm, RET_WIDTH), row),
                pl.BlockSpec((tm, ATT_WIDTH), row), pl.BlockSpec((tm, PLE_DIM), row)]
    in_specs += [_const_spec(w.shape) for w in weights]
    args = [x2d, r2d, a2d, p2d, *weights]
    scratch = [pltpu.VMEM((tm, D_MODEL), F32), pltpu.VMEM((tm, D_MODEL), BF16), pltpu.VMEM((tm, D_FF), BF16)]
    if shift:
        out_specs = [pl.BlockSpec((tm, D_MODEL), row),
                     pl.BlockSpec((1, CONV_W - 1, two_f), lambda i: (i // tiles_per_seq, 0, 0))]
        out_shape = [jax.ShapeDtypeStruct((n, D_MODEL), F32),
                     jax.ShapeDtypeStruct((n // (tm * tiles_per_seq), CONV_W - 1, two_f), F32)]
        scratch += [pltpu.VMEM((8 + tm, FF_CHUNK), F32), pltpu.VMEM((8 + tm, FF_CHUNK), F32),
                    pltpu.VMEM((8, two_f), F32)]
    else:
        in_specs += [pl.BlockSpec((tm, two_f), row)] * 2
        args += list(prefix)
        out_specs = [pl.BlockSpec((tm, D_MODEL), row), pl.BlockSpec((tm, two_f), row)]
        out_shape = [jax.ShapeDtypeStruct((n, D_MODEL), F32), jax.ShapeDtypeStruct((n, two_f), F32)]
    return pl.pallas_call(
        functools.partial(_post_kernel, tm=tm, tiles_per_seq=tiles_per_seq, shift=shift),
        grid=(n // tm,),
        in_specs=in_specs,
        out_specs=out_specs,
        out_shape=out_shape,
        scratch_shapes=scratch,
        compiler_params=pltpu.CompilerParams(dimension_semantics=("arbitrary",), vmem_limit_bytes=VMEM_LIMIT),
    )(*args)


def _sample_mix_kernel(qr_ref, kr_ref, vr_ref, gr_ref, st_ref, qa_ref, ka_ref, qat_ref, vat_ref,
                       kt_ref, vt_ref, r_ref, att_ref, nst_ref, s_s, pc_s):
    hrow = lax.broadcasted_iota(jnp.int32, (H_RET, RET_WIDTH), 0)
    hlane = lax.broadcasted_iota(jnp.int32, (H_RET, RET_WIDTH), 1) // DK_RET
    own = hrow == hlane

    qm = jnp.where(own, qr_ref[0], 0.0)
    km = jnp.where(own, kr_ref[0], 0.0)
    v8 = vr_ref[0]
    g8 = gr_ref[0]
    st = st_ref[0]
    lg8 = _select_log_gamma(lax.broadcasted_iota(jnp.int32, (H_RET, 1), 0))
    cross = jnp.dot(qm.astype(BF16), st.astype(BF16), preferred_element_type=F32) * jnp.exp(lg8)
    qk = jnp.sum(qm * km, axis=-1, keepdims=True)
    o = qk * v8 + cross
    lg_rows = _select_log_gamma(lax.broadcasted_iota(jnp.int32, (H_RET * DK_RET, 1), 0) // DK_RET)
    outer = lax.dot_general(km, v8, (((0,), (0,)), ((), ())), preferred_element_type=F32,
                            precision=lax.Precision.HIGHEST)
    nst_ref[0] = jnp.exp(lg_rows) * st + outer
    mu = jnp.mean(o, axis=-1, keepdims=True)
    xc = o - mu
    var = jnp.mean(xc * xc, axis=-1, keepdims=True)
    r_ref[0] = xc * lax.rsqrt(var + LN_EPS) * (g8 * _sigmoid(g8))

    n_past = kt_ref.shape[-1]
    s_new = jnp.sum(qa_ref[0] * ka_ref[0], axis=-1, keepdims=True) * ATT_SCALE
    qt = qat_ref[0] * ATT_SCALE
    for h in range(H_ATT):
        s_s[h:h + 1, :] = jnp.sum(kt_ref[h] * qt[:, h:h + 1], axis=0, keepdims=True)
    tok = lax.broadcasted_iota(jnp.int32, (H_ATT, n_past), 1)
    stats = []
    for window, dil in DILATIONS:
        lo = n_past - window
        sb = s_s[:, lo:] + jnp.where(tok[:, lo:] % dil == n_past % dil, 0.0, NEG_BIG)
        m = jnp.maximum(jnp.max(sb, axis=-1, keepdims=True), s_new)
        p = jnp.exp(sb - m)
        p_new = jnp.exp(s_new - m)
        den = jnp.sum(p, axis=-1, keepdims=True) + p_new
        stats.append((lo, p, p_new, den, m + jnp.log(den)))
    mx = jnp.maximum(jnp.maximum(stats[0][4], stats[1][4]), stats[2][4])
    es = [jnp.exp(st_[4] - mx) for st_ in stats]
    tot = es[0] + es[1] + es[2]
    coef = [e / (tot * st_[3]) for e, st_ in zip(es, stats)]
    w_new = coef[0] * stats[0][2] + coef[1] * stats[1][2] + coef[2] * stats[2][2]
    order = sorted(range(len(stats)), key=lambda i: stats[i][0])
    assert stats[order[0]][0] == 0
    pc_s[...] = coef[order[0]] * stats[order[0]][1]
    for i in order[1:]:
        lo = stats[i][0]
        pc_s[:, lo:] = pc_s[:, lo:] + coef[i] * stats[i][1]
    vt_new = vat_ref[0]
    for h in range(H_ATT):
        col = jnp.sum(vt_ref[h] * pc_s[h:h + 1, :], axis=-1, keepdims=True)
        att_ref[0, :, h:h + 1] = col + w_new[h:h + 1, :] * vt_new[:, h:h + 1]


def _sample_mix_call(qr, kr, vr, gr, state, qa, ka, va, cache_k, cache_v):
    nb = qr.shape[0]
    n_past = cache_k.shape[-1]
    assert all(w <= n_past and n_past % d == 0 for w, d in DILATIONS) and max(w for w, _ in DILATIONS) == n_past
    row3 = pl.BlockSpec((1, 1, RET_WIDTH), lambda b: (b, 0, 0))
    head3 = pl.BlockSpec((1, H_RET, DK_RET), lambda b: (b, 0, 0))
    col3 = pl.BlockSpec((1, HD_ATT, H_ATT), lambda b: (b, 0, 0))
    st_spec = pl.BlockSpec((1, H_RET * DK_RET, DK_RET), lambda b: (b, 0, 0))
    cache_spec = pl.BlockSpec((None, H_ATT, HD_ATT, n_past), lambda b: (b, 0, 0, 0))
    r3 = lambda a: a.reshape(nb, 1, RET_WIDTH)
    h3 = lambda a: a.reshape(nb, H_RET, DK_RET)
    t3 = lambda a: jnp.swapaxes(h3(a), 1, 2)
    r_s, att_t, nst = pl.pallas_call(
        _sample_mix_kernel,
        grid=(nb,),
        in_specs=[row3, row3, head3, head3, st_spec, head3, head3, col3, col3, cache_spec, cache_spec],
        out_specs=[head3, col3, st_spec],
        out_shape=[jax.ShapeDtypeStruct((nb, H_RET, DK_RET), F32),
                   jax.ShapeDtypeStruct((nb, HD_ATT, H_ATT), F32),
                   jax.ShapeDtypeStruct((nb, H_RET * DK_RET, DK_RET), F32)],
        scratch_shapes=[pltpu.VMEM((H_ATT, n_past), F32), pltpu.VMEM((H_ATT, n_past), F32)],
        compiler_params=pltpu.CompilerParams(dimension_semantics=("parallel",), vmem_limit_bytes=VMEM_LIMIT),
    )(r3(qr), r3(kr), h3(vr), h3(gr), state.reshape(nb, H_RET * DK_RET, DK_RET), h3(qa), h3(ka), t3(qa), t3(va),
      cache_k, cache_v)
    return r_s, jnp.swapaxes(att_t, 1, 2), nst


def kernel(x_prompt, x_sample, cache_k_win, cache_v_win, state_ret, state_conv, p_prompt, p_sample,
           ln_in_g, ln_in_b, w_in, w_out, ln1_g, ln1_b, w_up, conv_w, conv_b, w_down,
           w_ple_gate, w_ple_proj, ln2_g, ln2_b):
    B, S, _ = x_prompt.shape
    NB, T, _ = x_sample.shape
    assert T == 1 and w_in.shape[0] == DEPTH == 1 and S % PROJ_TILE == 0 and S % POST_TILE == 0
    two_f = 2 * D_FF
    vec = lambda a: a.reshape(1, -1)
    w_in_b = w_in[0].astype(BF16)
    post_w = (vec(ln_in_g), vec(ln_in_b), w_out[0].astype(BF16), vec(ln1_g[0]), vec(ln1_b[0]),
              w_up[0].astype(BF16), conv_w[0], vec(conv_b[0]), w_down[0].astype(BF16),
              w_ple_gate[0].astype(BF16), w_ple_proj[0].astype(BF16), vec(ln2_g[0]), vec(ln2_b[0]))

    pos_p = jnp.arange(S, dtype=jnp.int32)
    tabs_p = _rot_tables(pos_p, DK_RET, RET_ROPE_BASE) + _rot_tables(pos_p, ROT_DIMS, ATT_ROPE_THETA)
    xp = x_prompt.reshape(B * S, D_MODEL)
    qr, kr, vr, gr, qa, ka, va, ka_t, va_t = _proj_call(
        xp, vec(ln_in_g), vec(ln_in_b), w_in_b, tabs_p, PROJ_TILE, S // PROJ_TILE,
        (BF16, F32, BF16, F32, F32, F32, F32), seq_for_transposed=S)
    r_p, ret_fin = _ret_call(qr, kr, vr, gr, B, S)
    att_p = _att_call(qa, ka, va, B, S)
    y_p, conv_p = _post_call(xp, r_p, att_p, p_prompt[0].reshape(B * S, PLE_DIM), post_w,
                             POST_TILE, S // POST_TILE)
    keep = min(WINDOW_MAX, S)
    k_win_p = jnp.transpose(ka_t, (0, 3, 1, 2))[:, S - keep:]
    v_win_p = jnp.transpose(va_t, (0, 3, 1, 2))[:, S - keep:]

    pos_s = jnp.full((NB,), PAST_LEN, jnp.int32)
    tabs_s = _rot_tables(pos_s, DK_RET, RET_ROPE_BASE) + _rot_tables(pos_s, ROT_DIMS, ATT_ROPE_THETA)
    xs = x_sample.reshape(NB, D_MODEL)
    sqr, skr, svr, sgr, sqa, ska, sva = _proj_call(xs, vec(ln_in_g), vec(ln_in_b), w_in_b, tabs_s, NB, 1,
                                                   (F32,) * 7)
    r_s, att_s, nst = _sample_mix_call(sqr, skr, svr, sgr, state_ret[0], sqa, ska, sva,
                                       jnp.transpose(cache_k_win[0], (0, 2, 3, 1)),
                                       jnp.transpose(cache_v_win[0], (0, 2, 3, 1)))
    y_s, u_s = _post_call(xs, r_s.reshape(NB, RET_WIDTH), att_s.reshape(NB, ATT_WIDTH), p_sample[0].reshape(NB, PLE_DIM),
                          post_w, NB, 1, prefix=(state_conv[0][:, 0], state_conv[0][:, 1]))
    conv_s = jnp.stack([state_conv[0][:, 1], u_s], axis=1)

    return (y_p.reshape(B, S, D_MODEL), y_s.reshape(NB, 1, D_MODEL),
            k_win_p[None], v_win_p[None], ret_fin[None], conv_p[None],
            ska.reshape(1, NB, 1, H_ATT, HD_ATT), sva.reshape(1, NB, 1, H_ATT, HD_ATT),
            nst.reshape(1, NB, H_RET, DK_RET, DK_RET), conv_s[None])
```

```python
import functools
import math

import numpy as np
import jax
import jax.numpy as jnp
from jax import lax
from jax.experimental import pallas as pl
from jax.experimental.pallas import tpu as pltpu

F32 = jnp.float32
BF16 = jnp.bfloat16

D_MODEL = 1024
PAST_LEN = 16384
H_RET = 8
DK_RET = 64
RET_WIDTH = 512
RET_CHUNK = 128
RET_ROPE_BASE = 10000.0
H_ATT = 8
HD_ATT = 64
ATT_WIDTH = 512
ATT_ROPE_THETA = 500000.0
ROT_DIMS = HD_ATT // 4
DILATIONS = ((128, 1), (512, 4), (2048, 16))
WINDOW_MAX = 2048
Q_BLOCK = 128
PROJ_WIDTH = 4 * RET_WIDTH + 3 * ATT_WIDTH
D_FF = 2816
CONV_W = 3
PLE_DIM = 256
LN_EPS = 1e-5
DEPTH = 1
ALPHA = (2 * DEPTH) ** 0.25

LANES = 128
HEADS_PER_TILE = LANES // DK_RET
NEG_BIG = -1e30
LOG_GAMMA = tuple(math.log(1.0 - 2.0 ** (-5.0 - h)) for h in range(H_RET))
ATT_SCALE = HD_ATT ** -0.5
VMEM_LIMIT = 56 * 1024 * 1024

PROJ_TILE = 512
POST_TILE = 512
FF_CHUNK = 256
DOWN_GROUP = 3


def _layer_norm(x, g, b):
    mu = jnp.mean(x, axis=-1, keepdims=True)
    xc = x - mu
    var = jnp.mean(xc * xc, axis=-1, keepdims=True)
    return xc * lax.rsqrt(var + LN_EPS) * g + b


def _sigmoid(x):
    return 1.0 / (1.0 + jnp.exp(-x))


def _gelu_tanh(x):
    return 0.5 * x * (1.0 + jnp.tanh(math.sqrt(2.0 / math.pi) * (x + 0.044715 * (x * x * x))))


def _select_log_gamma(head_idx):
    out = jnp.zeros(head_idx.shape, F32)
    for h in range(H_RET):
        out = jnp.where(head_idx == h, LOG_GAMMA[h], out)
    return out


def _const_spec(shape):
    nd = len(shape)
    return pl.BlockSpec(shape, lambda *_: (0,) * nd, pipeline_mode=pl.Buffered(1))


def _rotate(z, cos, s_lo, s_hi, half):
    up = pltpu.roll(z, LANES - half, 1)
    dn = pltpu.roll(z, half, 1)
    return z * cos + up * s_lo + dn * s_hi


def _proj_kernel(x_ref, g_ref, b_ref, w_ref, cr_ref, slr_ref, shr_ref, ca_ref, sla_ref, sha_ref,
                 qr_ref, kr_ref, vr_ref, gr_ref, qa_ref, ka_ref, va_ref, *t_refs):
    hb = _layer_norm(x_ref[...], g_ref[...], b_ref[...]).astype(BF16)
    outs = (qr_ref, kr_ref, vr_ref, gr_ref, qa_ref, ka_ref, va_ref)
    for grp, o_ref in enumerate(outs):
        zg = jnp.dot(hb, w_ref[:, grp * RET_WIDTH:(grp + 1) * RET_WIDTH], preferred_element_type=F32)
        for j in range(RET_WIDTH // LANES):
            z = zg[:, j * LANES:(j + 1) * LANES]
            if grp in (0, 1):
                z = _rotate(z, cr_ref[...], slr_ref[...], shr_ref[...], DK_RET // 2)
                if grp == 1:
                    z = z * (DK_RET ** -0.5)
            elif grp in (4, 5):
                z = _rotate(z, ca_ref[...], sla_ref[...], sha_ref[...], ROT_DIMS // 2)
            o_ref[:, j * LANES:(j + 1) * LANES] = z.astype(o_ref.dtype)
            if t_refs and grp in (5, 6):
                zt = z.T
                for hh in range(HEADS_PER_TILE):
                    t_refs[grp - 5][0, HEADS_PER_TILE * j + hh] = zt[hh * HD_ATT:(hh + 1) * HD_ATT]


def _rot_tables(pos, n_rot, base):
    half = n_rot // 2
    inv = base ** (-jnp.arange(half, dtype=F32) / half)
    ang = pos.astype(F32)[:, None] * inv[None, :]
    cos, sin = jnp.cos(ang), jnp.sin(ang)
    l = np.arange(LANES) % DK_RET
    idx = np.where(l < half, l, np.where(l < n_rot, l - half, 0))
    lo = jnp.asarray(l < half)
    hi = jnp.asarray((l >= half) & (l < n_rot))
    cos_f = jnp.where(jnp.asarray(l < n_rot), cos[:, idx], 1.0)
    s_lo = jnp.where(lo, -sin[:, idx], 0.0)
    s_hi = jnp.where(hi, sin[:, idx], 0.0)
    return cos_f, s_lo, s_hi


def _proj_call(x2d, ln_g, ln_b, w_in_b, tabs, tm, tab_blocks, out_dtypes, seq_for_transposed=None):
    n = x2d.shape[0]
    row = lambda i: (i, 0)
    tab = lambda i: (i % tab_blocks, 0)
    tab_spec = pl.BlockSpec((tm, LANES), tab)
    out_specs = [pl.BlockSpec((tm, RET_WIDTH), row)] * 7
    out_shape = [jax.ShapeDtypeStruct((n, RET_WIDTH), dt) for dt in out_dtypes]
    if seq_for_transposed is not None:
        tps = seq_for_transposed // tm
        out_specs += [pl.BlockSpec((1, H_ATT, HD_ATT, tm), lambda i: (i // tps, 0, 0, i % tps))] * 2
        out_shape += [jax.ShapeDtypeStruct((n // seq_for_transposed, H_ATT, HD_ATT, seq_for_transposed), F32)] * 2
    return pl.pallas_call(
        _proj_kernel,
        grid=(n // tm,),
        in_specs=[pl.BlockSpec((tm, D_MODEL), row), _const_spec((1, D_MODEL)), _const_spec((1, D_MODEL)),
                  _const_spec((D_MODEL, PROJ_WIDTH))] + [tab_spec] * 6,
        out_specs=out_specs,
        out_shape=out_shape,
        compiler_params=pltpu.CompilerParams(dimension_semantics=("parallel",), vmem_limit_bytes=VMEM_LIMIT),
    )(x2d, ln_g, ln_b, w_in_b, *tabs)


RET_UNROLL = 16


def _ret_kernel(q_ref, k_ref, v_ref, g_ref, r_ref, st_ref, *, n_chunks):
    C = RET_CHUNK
    hp = pl.program_id(1)
    lane1 = lax.broadcasted_iota(jnp.int32, (1, LANES), 1)
    lg_lane = _select_log_gamma(HEADS_PER_TILE * hp + (lane1 >= DK_RET).astype(jnp.int32))

    ri = lax.broadcasted_iota(jnp.int32, (2 * C, C), 0)
    ci = lax.broadcasted_iota(jnp.int32, (2 * C, C), 1)
    rel = (jnp.where(ri >= C, ri - C, ri) - ci).astype(F32)
    lg_rows = _select_log_gamma(HEADS_PER_TILE * hp + (ri >= C).astype(jnp.int32))
    decay = jnp.where(rel >= 0, jnp.exp(lg_rows * jnp.maximum(rel, 0.0)), 0.0)

    tok = lax.broadcasted_iota(jnp.int32, (C, LANES), 0).astype(F32)
    cross_dec = jnp.exp(lg_lane * (tok + 1.0))
    k_dec = jnp.exp(lg_lane * (C - 1.0 - tok))
    sr = lax.broadcasted_iota(jnp.int32, (LANES, LANES), 0)
    sc = lax.broadcasted_iota(jnp.int32, (LANES, LANES), 1)
    same_head = ((sr >= DK_RET) == (sc >= DK_RET)).astype(F32)
    state_dec = jnp.exp(_select_log_gamma(HEADS_PER_TILE * hp + (sr >= DK_RET).astype(jnp.int32)) * float(C))

    def body(c, state):
        first = lax.broadcasted_iota(jnp.int32, (1, LANES), 1) < DK_RET
        off = pl.multiple_of(c * C, C)
        q = q_ref[pl.ds(off, C), :]
        k = k_ref[pl.ds(off, C), :]
        v = v_ref[pl.ds(off, C), :]
        g = g_ref[pl.ds(off, C), :]
        zero = jnp.zeros_like(q)
        qs = jnp.concatenate([jnp.where(first, q, zero), jnp.where(first, zero, q)], axis=0)
        s = lax.dot_general(qs, k.astype(BF16), (((1,), (1,)), ((), ())), preferred_element_type=F32)
        pv = jnp.dot((s * decay).astype(BF16), v, preferred_element_type=F32)
        inner = jnp.where(first, pv[:C], pv[C:])
        cross = jnp.dot(q, state.astype(BF16), preferred_element_type=F32) * cross_dec
        o = inner + cross
        upd = lax.dot_general((k * k_dec).astype(BF16), v, (((0,), (0,)), ((), ())),
                              preferred_element_type=F32)
        new_state = state_dec * state + same_head * upd
        s_a = jnp.sum(jnp.where(first, o, 0.0), axis=-1, keepdims=True)
        s_b = jnp.sum(jnp.where(first, 0.0, o), axis=-1, keepdims=True)
        xc = o - jnp.where(first, s_a, s_b) * (1.0 / DK_RET)
        sq = xc * xc
        v_a = jnp.sum(jnp.where(first, sq, 0.0), axis=-1, keepdims=True)
        v_b = jnp.sum(jnp.where(first, 0.0, sq), axis=-1, keepdims=True)
        rn = xc * lax.rsqrt(jnp.where(first, v_a, v_b) * (1.0 / DK_RET) + LN_EPS)
        r_ref[pl.ds(off, C), :] = (rn * (g * _sigmoid(g))).astype(r_ref.dtype)
        return new_state

    state = lax.fori_loop(0, n_chunks, body, jnp.zeros((LANES, LANES), F32), unroll=RET_UNROLL)
    st_ref[0, 0] = state[:DK_RET, :DK_RET]
    st_ref[0, 1] = state[DK_RET:, DK_RET:]


def _ret_call(qr, kr, vr, gr, batch, seq):
    blk = pl.BlockSpec((seq, LANES), lambda b, p: (b, p))
    n_pairs = H_RET // HEADS_PER_TILE
    return pl.pallas_call(
        functools.partial(_ret_kernel, n_chunks=seq // RET_CHUNK),
        grid=(batch, n_pairs),
        in_specs=[blk] * 4,
        out_specs=[blk, pl.BlockSpec((1, HEADS_PER_TILE, DK_RET, DK_RET), lambda b, p: (b, p, 0, 0))],
        out_shape=[jax.ShapeDtypeStruct((batch * seq, RET_WIDTH), BF16),
                   jax.ShapeDtypeStruct((batch, H_RET, DK_RET, DK_RET), F32)],
        compiler_params=pltpu.CompilerParams(dimension_semantics=("parallel", "parallel"),
                                             vmem_limit_bytes=VMEM_LIMIT),
    )(qr, kr, vr, gr)


def _needs_pitch(dil):
    return dil % 8 == 0


def _att_kernel(q_ref, k_ref, v_ref, o_ref, num_ref, den_ref, max_ref, qp_ref, kp_ref, vp_ref, *, seq):
    QB = Q_BLOCK
    pitched = [dil for _, dil in DILATIONS if _needs_pitch(dil)]
    assert len(pitched) <= 1
    for dil in pitched:
        for g in range(seq // dil):
            for src, dst in ((q_ref, qp_ref), (k_ref, kp_ref), (v_ref, vp_ref)):
                dst[g * (dil + 1):g * (dil + 1) + dil, :] = src[g * dil:(g + 1) * dil, :]
    ri = lax.broadcasted_iota(jnp.int32, (2 * QB, 2 * QB), 0)
    ci = lax.broadcasted_iota(jnp.int32, (2 * QB, 2 * QB), 1)
    qi = jnp.where(ri >= QB, ri - QB, ri)
    valid_band = ((ci < QB) & (ci >= qi)) | ((ci >= QB) & (ci - QB <= qi))
    bias_band = jnp.where(valid_band, 0.0, NEG_BIG)
    rd = lax.broadcasted_iota(jnp.int32, (2 * QB, QB), 0)
    cd = lax.broadcasted_iota(jnp.int32, (2 * QB, QB), 1)
    bias_diag = jnp.where(cd <= jnp.where(rd >= QB, rd - QB, rd), 0.0, NEG_BIG)
    ones = jnp.ones((2 * QB, LANES), BF16)

    def unit(br, dil, q_start, k_start, n_keys, bias):
        lane = lax.broadcasted_iota(jnp.int32, (1, LANES), 1)
        first = lane < HD_ATT
        rows_q = pl.ds(q_start, QB, stride=dil) if dil > 1 else pl.ds(q_start, QB)
        if _needs_pitch(dil):
            src_q, src_k, src_v = qp_ref, kp_ref, vp_ref
            rows_qs = pl.ds(q_start, QB, stride=dil + 1)
            rows_k = pl.ds(k_start, n_keys, stride=dil + 1)
        else:
            src_q, src_k, src_v = q_ref, k_ref, v_ref
            rows_qs = rows_q
            rows_k = pl.ds(k_start, n_keys, stride=dil) if dil > 1 else pl.ds(k_start, n_keys)
        q = src_q[rows_qs, :] * ATT_SCALE
        qs = jnp.concatenate([jnp.where(first, q, 0.0), jnp.where(first, 0.0, q)], axis=0).astype(BF16)
        kk = src_k[rows_k, :].astype(BF16)
        vv = src_v[rows_k, :].astype(BF16)
        s = lax.dot_general(qs, kk, (((1,), (1,)), ((), ())), preferred_element_type=F32)
        s = s + bias
        m = jnp.max(s, axis=-1, keepdims=True)
        p = jnp.exp(s - m).astype(BF16)
        pv = jnp.dot(p, jnp.concatenate([vv, ones[:n_keys]], axis=1), preferred_element_type=F32)
        mb = jnp.broadcast_to(m, (2 * QB, LANES))
        num_ref[br, rows_q, :] = jnp.where(first, pv[:QB, :LANES], pv[QB:, :LANES])
        den_ref[br, rows_q, :] = jnp.where(first, pv[:QB, LANES:], pv[QB:, LANES:])
        max_ref[br, rows_q, :] = jnp.where(first, mb[:QB], mb[QB:])

    for br, (window, dil) in enumerate(DILATIONS):
        assert window // dil == QB
        sub_len = seq // dil
        n_blocks = sub_len // QB

        assert not (_needs_pitch(dil) and n_blocks > 1)
        for r in range(dil):
            unit(br, dil, r, r, QB, bias_diag)
            for n in range(1, n_blocks):
                k_start = r + dil * (n - 1) * QB
                unit(br, dil, k_start + dil * QB, k_start, 2 * QB, bias_band)

    def combine(i, carry):
        rows = pl.ds(pl.multiple_of(i * 256, 256), 256)
        m0, m1, m2 = max_ref[0, rows, :], max_ref[1, rows, :], max_ref[2, rows, :]
        mx = jnp.maximum(jnp.maximum(m0, m1), m2)
        e0, e1, e2 = jnp.exp(m0 - mx), jnp.exp(m1 - mx), jnp.exp(m2 - mx)
        num = e0 * num_ref[0, rows, :] + e1 * num_ref[1, rows, :] + e2 * num_ref[2, rows, :]
        den = e0 * den_ref[0, rows, :] + e1 * den_ref[1, rows, :] + e2 * den_ref[2, rows, :]
        o_ref[rows, :] = (num / den).astype(o_ref.dtype)
        return carry

    lax.fori_loop(0, seq // 256, combine, 0)


def _att_call(qa, ka, va, batch, seq):
    blk = pl.BlockSpec((seq, LANES), lambda b, p: (b, p))
    n_br = len(DILATIONS)
    pitched_rows = max([seq // dil * (dil + 1) for _, dil in DILATIONS if _needs_pitch(dil)] + [8])
    return pl.pallas_call(
        functools.partial(_att_kernel, seq=seq),
        grid=(batch, H_ATT // HEADS_PER_TILE),
        in_specs=[blk] * 3,
        out_specs=blk,
        out_shape=jax.ShapeDtypeStruct((batch * seq, ATT_WIDTH), BF16),
        scratch_shapes=[pltpu.VMEM((n_br, seq, LANES), F32)] * 3 + [pltpu.VMEM((pitched_rows, LANES), F32)] * 3,
        compiler_params=pltpu.CompilerParams(dimension_semantics=("parallel", "parallel"),
                                             vmem_limit_bytes=VMEM_LIMIT),
    )(qa, ka, va)


def _post_kernel(*refs, tm, tiles_per_seq, shift):
    (x_ref, r_ref, a_ref, p_ref, lng_ref, lnb_ref, wo_ref, g1_ref, b1_ref, wup_ref, cw_ref, cb_ref,
     wdn_ref, wpg_ref, wpp_ref, g2_ref, b2_ref) = refs[:17]
    if shift:
        y_ref, cs_ref, h1b_s, halo_s = refs[17:]
    else:
        pre0_ref, pre1_ref, y_ref, cs_ref, h1b_s = refs[17:]

    h = _layer_norm(x_ref[...], lng_ref[...], lnb_ref[...])
    mix = (jnp.dot(r_ref[...].astype(BF16), wo_ref[:RET_WIDTH, :], preferred_element_type=F32)
           + jnp.dot(a_ref[...].astype(BF16), wo_ref[RET_WIDTH:, :], preferred_element_type=F32))
    h1 = _layer_norm(ALPHA * h + mix, g1_ref[...], b1_ref[...])
    h1b_s[...] = h1.astype(BF16)
    gate = _sigmoid(jnp.dot(h1b_s[...], wpg_ref[...], preferred_element_type=F32))
    y_ref[...] = ALPHA * h1 + gate * jnp.dot(p_ref[...].astype(BF16), wpp_ref[...], preferred_element_type=F32)

    if shift:
        @pl.when(pl.program_id(0) % tiles_per_seq == 0)
        def _():
            halo_s[...] = jnp.zeros_like(halo_s)
        row8 = lax.broadcasted_iota(jnp.int32, (8, FF_CHUNK), 0)

    n_chunks = D_FF // FF_CHUNK
    partials, acts = [], []
    for j in range(n_chunks):
        conv = []
        for part in (0, 1):
            c0 = part * D_FF + j * FF_CHUNK
            cols = slice(c0, c0 + FF_CHUNK)
            u = jnp.dot(h1b_s[...], wup_ref[:, cols], preferred_element_type=F32)
            if shift:
                halo = halo_s[:, cols]
                halo_s[:, cols] = u[tm - 8:, :]
                cs_ref[0, :, cols] = u[tm - (CONV_W - 1):, :]

                def shifted(k, u=u, halo=halo):
                    rolled = pltpu.roll(u, k, 0)
                    top = jnp.where(row8 < k, pltpu.roll(halo, k, 0), rolled[0:8])
                    return jnp.concatenate([top, rolled[8:]], axis=0)

                prev2, prev1 = shifted(2), shifted(1)
            else:
                cs_ref[:, cols] = u
                prev2 = pre0_ref[:, cols]
                prev1 = pre1_ref[:, cols]
            conv.append(cb_ref[:, cols] + cw_ref[0:1, cols] * prev2 + cw_ref[1:2, cols] * prev1
                        + cw_ref[2:3, cols] * u)
        acts.append((conv[0] * _gelu_tanh(conv[1])).astype(BF16))
        if len(acts) == DOWN_GROUP or j == n_chunks - 1:
            r0 = (j + 1 - len(acts)) * FF_CHUNK
            partials.append(jnp.dot(jnp.concatenate(acts, axis=1), wdn_ref[r0:(j + 1) * FF_CHUNK, :],
                                    preferred_element_type=F32))
            acts = []

    y_ref[...] = _layer_norm(y_ref[...] + sum(partials[1:], partials[0]), g2_ref[...], b2_ref[...])


def _post_call(x2d, r2d, a2d, p2d, weights, tm, tiles_per_seq, prefix=None):
    n = x2d.shape[0]
    shift = prefix is None
    two_f = 2 * D_FF
    row = lambda i: (i, 0)
    in_specs = [pl.BlockSpec((tm, D_MODEL), row), pl.BlockSpec((tm, RET_WIDTH), row),
                pl.BlockSpec((tm, ATT_WIDTH), row), pl.BlockSpec((tm, PLE_DIM), row)]
    in_specs += [_const_spec(w.shape) for w in weights]
    args = [x2d, r2d, a2d, p2d, *weights]
    scratch = [pltpu.VMEM((tm, D_MODEL), BF16)]
    if shift:
        out_specs = [pl.BlockSpec((tm, D_MODEL), row),
                     pl.BlockSpec((1, CONV_W - 1, two_f), lambda i: (i // tiles_per_seq, 0, 0))]
        out_shape = [jax.ShapeDtypeStruct((n, D_MODEL), F32),
                     jax.ShapeDtypeStruct((n // (tm * tiles_per_seq), CONV_W - 1, two_f), F32)]
        scratch += [pltpu.VMEM((8, two_f), F32)]
    else:
        in_specs += [pl.BlockSpec((tm, two_f), row)] * 2
        args += list(prefix)
        out_specs = [pl.BlockSpec((tm, D_MODEL), row), pl.BlockSpec((tm, two_f), row)]
        out_shape = [jax.ShapeDtypeStruct((n, D_MODEL), F32), jax.ShapeDtypeStruct((n, two_f), F32)]
    return pl.pallas_call(
        functools.partial(_post_kernel, tm=tm, tiles_per_seq=tiles_per_seq, shift=shift),
        grid=(n // tm,),
        in_specs=in_specs,
        out_specs=out_specs,
        out_shape=out_shape,
        scratch_shapes=scratch,
        compiler_params=pltpu.CompilerParams(dimension_semantics=("arbitrary",), vmem_limit_bytes=VMEM_LIMIT),
    )(*args)


def _sample_mix_kernel(qr_ref, kr_ref, vr_ref, gr_ref, st_ref, qa_ref, ka_ref, qat_ref, vat_ref,
                       kt_ref, vt_ref, r_ref, att_ref, nst_ref, s_s, pc_s):
    hrow = lax.broadcasted_iota(jnp.int32, (H_RET, RET_WIDTH), 0)
    hlane = lax.broadcasted_iota(jnp.int32, (H_RET, RET_WIDTH), 1) // DK_RET
    own = hrow == hlane

    qm = jnp.where(own, qr_ref[0], 0.0)
    km = jnp.where(own, kr_ref[0], 0.0)
    v8 = vr_ref[0]
    g8 = gr_ref[0]
    st = st_ref[0]
    lg8 = _select_log_gamma(lax.broadcasted_iota(jnp.int32, (H_RET, 1), 0))
    cross = jnp.dot(qm.astype(BF16), st.astype(BF16), preferred_element_type=F32) * jnp.exp(lg8)
    qk = jnp.sum(qm * km, axis=-1, keepdims=True)
    o = qk * v8 + cross
    lg_rows = _select_log_gamma(lax.broadcasted_iota(jnp.int32, (H_RET * DK_RET, 1), 0) // DK_RET)
    outer = lax.dot_general(km, v8, (((0,), (0,)), ((), ())), preferred_element_type=F32,
                            precision=lax.Precision.HIGHEST)
    nst_ref[0] = jnp.exp(lg_rows) * st + outer
    mu = jnp.mean(o, axis=-1, keepdims=True)
    xc = o - mu
    var = jnp.mean(xc * xc, axis=-1, keepdims=True)
    r_ref[0] = xc * lax.rsqrt(var + LN_EPS) * (g8 * _sigmoid(g8))

    n_past = kt_ref.shape[-1]
    s_new = jnp.sum(qa_ref[0] * ka_ref[0], axis=-1, keepdims=True) * ATT_SCALE
    qt = qat_ref[0] * ATT_SCALE
    for h in range(H_ATT):
        s_s[h:h + 1, :] = jnp.sum(kt_ref[h] * qt[:, h:h + 1], axis=0, keepdims=True)
    tok = lax.broadcasted_iota(jnp.int32, (H_ATT, n_past), 1)
    stats = []
    for window, dil in DILATIONS:
        lo = n_past - window
        sb = s_s[:, lo:] + jnp.where(tok[:, lo:] % dil == n_past % dil, 0.0, NEG_BIG)
        m = jnp.maximum(jnp.max(sb, axis=-1, keepdims=True), s_new)
        p = jnp.exp(sb - m)
        p_new = jnp.exp(s_new - m)
        den = jnp.sum(p, axis=-1, keepdims=True) + p_new
        stats.append((lo, p, p_new, den, m + jnp.log(den)))
    mx = jnp.maximum(jnp.maximum(stats[0][4], stats[1][4]), stats[2][4])
    es = [jnp.exp(st_[4] - mx) for st_ in stats]
    tot = es[0] + es[1] + es[2]
    coef = [e / (tot * st_[3]) for e, st_ in zip(es, stats)]
    w_new = coef[0] * stats[0][2] + coef[1] * stats[1][2] + coef[2] * stats[2][2]
    order = sorted(range(len(stats)), key=lambda i: stats[i][0])
    assert stats[order[0]][0] == 0
    pc_s[...] = coef[order[0]] * stats[order[0]][1]
    for i in order[1:]:
        lo = stats[i][0]
        pc_s[:, lo:] = pc_s[:, lo:] + coef[i] * stats[i][1]
    vt_new = vat_ref[0]
    for h in range(H_ATT):
        col = jnp.sum(vt_ref[h] * pc_s[h:h + 1, :], axis=-1, keepdims=True)
        att_ref[0, :, h:h + 1] = col + w_new[h:h + 1, :] * vt_new[:, h:h + 1]


def _sample_mix_call(qr, kr, vr, gr, state, qa, ka, va, cache_k, cache_v):
    nb = qr.shape[0]
    n_past = cache_k.shape[-1]
    assert all(w <= n_past and n_past % d == 0 for w, d in DILATIONS) and max(w for w, _ in DILATIONS) == n_past
    row3 = pl.BlockSpec((1, 1, RET_WIDTH), lambda b: (b, 0, 0))
    head3 = pl.BlockSpec((1, H_RET, DK_RET), lambda b: (b, 0, 0))
    col3 = pl.BlockSpec((1, HD_ATT, H_ATT), lambda b: (b, 0, 0))
    st_spec = pl.BlockSpec((1, H_RET * DK_RET, DK_RET), lambda b: (b, 0, 0))
    cache_spec = pl.BlockSpec((None, H_ATT, HD_ATT, n_past), lambda b: (b, 0, 0, 0))
    r3 = lambda a: a.reshape(nb, 1, RET_WIDTH)
    h3 = lambda a: a.reshape(nb, H_RET, DK_RET)
    t3 = lambda a: jnp.swapaxes(h3(a), 1, 2)
    r_s, att_t, nst = pl.pallas_call(
        _sample_mix_kernel,
        grid=(nb,),
        in_specs=[row3, row3, head3, head3, st_spec, head3, head3, col3, col3, cache_spec, cache_spec],
        out_specs=[head3, col3, st_spec],
        out_shape=[jax.ShapeDtypeStruct((nb, H_RET, DK_RET), F32),
                   jax.ShapeDtypeStruct((nb, HD_ATT, H_ATT), F32),
                   jax.ShapeDtypeStruct((nb, H_RET * DK_RET, DK_RET), F32)],
        scratch_shapes=[pltpu.VMEM((H_ATT, n_past), F32), pltpu.VMEM((H_ATT, n_past), F32)],
        compiler_params=pltpu.CompilerParams(dimension_semantics=("parallel",), vmem_limit_bytes=VMEM_LIMIT),
    )(r3(qr), r3(kr), h3(vr), h3(gr), state.reshape(nb, H_RET * DK_RET, DK_RET), h3(qa), h3(ka), t3(qa), t3(va),
      cache_k, cache_v)
    return r_s, jnp.swapaxes(att_t, 1, 2), nst


def kernel(x_prompt, x_sample, cache_k_win, cache_v_win, state_ret, state_conv, p_prompt, p_sample,
           ln_in_g, ln_in_b, w_in, w_out, ln1_g, ln1_b, w_up, conv_w, conv_b, w_down,
           w_ple_gate, w_ple_proj, ln2_g, ln2_b):
    B, S, _ = x_prompt.shape
    NB, T, _ = x_sample.shape
    assert T == 1 and w_in.shape[0] == DEPTH == 1 and S % PROJ_TILE == 0 and S % POST_TILE == 0
    two_f = 2 * D_FF
    vec = lambda a: a.reshape(1, -1)
    w_in_b = w_in[0].astype(BF16)
    post_w = (vec(ln_in_g), vec(ln_in_b), w_out[0].astype(BF16), vec(ln1_g[0]), vec(ln1_b[0]),
              w_up[0].astype(BF16), conv_w[0], vec(conv_b[0]), w_down[0].astype(BF16),
              w_ple_gate[0].astype(BF16), w_ple_proj[0].astype(BF16), vec(ln2_g[0]), vec(ln2_b[0]))

    pos_p = jnp.arange(S, dtype=jnp.int32)
    tabs_p = _rot_tables(pos_p, DK_RET, RET_ROPE_BASE) + _rot_tables(pos_p, ROT_DIMS, ATT_ROPE_THETA)
    xp = x_prompt.reshape(B * S, D_MODEL)
    qr, kr, vr, gr, qa, ka, va, ka_t, va_t = _proj_call(
        xp, vec(ln_in_g), vec(ln_in_b), w_in_b, tabs_p, PROJ_TILE, S // PROJ_TILE,
        (BF16, F32, BF16, F32, F32, F32, F32), seq_for_transposed=S)
    r_p, ret_fin = _ret_call(qr, kr, vr, gr, B, S)
    att_p = _att_call(qa, ka, va, B, S)
    y_p, conv_p = _post_call(xp, r_p, att_p, p_prompt[0].reshape(B * S, PLE_DIM), post_w,
                             POST_TILE, S // POST_TILE)
    keep = min(WINDOW_MAX, S)
    k_win_p = jnp.transpose(ka_t, (0, 3, 1, 2))[:, S - keep:]
    v_win_p = jnp.transpose(va_t, (0, 3, 1, 2))[:, S - keep:]

    pos_s = jnp.full((NB,), PAST_LEN, jnp.int32)
    tabs_s = _rot_tables(pos_s, DK_RET, RET_ROPE_BASE) + _rot_tables(pos_s, ROT_DIMS, ATT_ROPE_THETA)
    xs = x_sample.reshape(NB, D_MODEL)
    sqr, skr, svr, sgr, sqa, ska, sva = _proj_call(xs, vec(ln_in_g), vec(ln_in_b), w_in_b, tabs_s, NB, 1,
                                                   (F32,) * 7)
    r_s, att_s, nst = _sample_mix_call(sqr, skr, svr, sgr, state_ret[0], sqa, ska, sva,
                                       jnp.transpose(cache_k_win[0], (0, 2, 3, 1)),
                                       jnp.transpose(cache_v_win[0], (0, 2, 3, 1)))
    y_s, u_s = _post_call(xs, r_s.reshape(NB, RET_WIDTH), att_s.reshape(NB, ATT_WIDTH), p_sample[0].reshape(NB, PLE_DIM),
                          post_w, NB, 1, prefix=(state_conv[0][:, 0], state_conv[0][:, 1]))
    conv_s = jnp.stack([state_conv[0][:, 1], u_s], axis=1)

    return (y_p.reshape(B, S, D_MODEL), y_s.reshape(NB, 1, D_MODEL),
            k_win_p[None], v_win_p[None], ret_fin[None], conv_p[None],
            ska.reshape(1, NB, 1, H_ATT, HD_ATT), sva.reshape(1, NB, 1, H_ATT, HD_ATT),
            nst.reshape(1, NB, H_RET, DK_RET, DK_RET), conv_s[None])
```

```python
import functools
import math

import numpy as np
import jax
import jax.numpy as jnp
from jax import lax
from jax.experimental import pallas as pl
from jax.experimental.pallas import tpu as pltpu

F32 = jnp.float32
BF16 = jnp.bfloat16

D_MODEL = 1024
PAST_LEN = 16384
H_RET = 8
DK_RET = 64
RET_WIDTH = 512
RET_CHUNK = 128
RET_ROPE_BASE = 10000.0
H_ATT = 8
HD_ATT = 64
ATT_WIDTH = 512
ATT_ROPE_THETA = 500000.0
ROT_DIMS = HD_ATT // 4
DILATIONS = ((128, 1), (512, 4), (2048, 16))
WINDOW_MAX = 2048
Q_BLOCK = 128
PROJ_WIDTH = 4 * RET_WIDTH + 3 * ATT_WIDTH
D_FF = 2816
CONV_W = 3
PLE_DIM = 256
LN_EPS = 1e-5
DEPTH = 1
ALPHA = (2 * DEPTH) ** 0.25

LANES = 128
HEADS_PER_TILE = LANES // DK_RET
NEG_BIG = -1e30
LOG_GAMMA = tuple(math.log(1.0 - 2.0 ** (-5.0 - h)) for h in range(H_RET))
ATT_SCALE = HD_ATT ** -0.5
VMEM_LIMIT = 56 * 1024 * 1024

PROJ_TILE = 512
POST_TILE = 512
FF_CHUNK = 256
DOWN_GROUP = 3
UP_K_SPLIT = 2
ACT_GROUP_ROWS = 128
ACT_STEPS_PER_UP_STEP = 3


def _layer_norm(x, g, b):
    mu = jnp.mean(x, axis=-1, keepdims=True)
    xc = x - mu
    var = jnp.mean(xc * xc, axis=-1, keepdims=True)
    return xc * lax.rsqrt(var + LN_EPS) * g + b


def _sigmoid(x):
    return 1.0 / (1.0 + jnp.exp(-x))


def _gelu_tanh(x):
    return 0.5 * x * (1.0 + jnp.tanh(math.sqrt(2.0 / math.pi) * (x + 0.044715 * (x * x * x))))


def _select_log_gamma(head_idx):
    out = jnp.zeros(head_idx.shape, F32)
    for h in range(H_RET):
        out = jnp.where(head_idx == h, LOG_GAMMA[h], out)
    return out


def _const_spec(shape):
    nd = len(shape)
    return pl.BlockSpec(shape, lambda *_: (0,) * nd, pipeline_mode=pl.Buffered(1))


def _rotate(z, cos, s_lo, s_hi, half):
    up = pltpu.roll(z, LANES - half, 1)
    dn = pltpu.roll(z, half, 1)
    return z * cos + up * s_lo + dn * s_hi


def _proj_kernel(x_ref, g_ref, b_ref, w_ref, cr_ref, slr_ref, shr_ref, ca_ref, sla_ref, sha_ref,
                 qr_ref, kr_ref, vr_ref, gr_ref, qa_ref, ka_ref, va_ref, *t_refs):
    hb = _layer_norm(x_ref[...], g_ref[...], b_ref[...]).astype(BF16)
    outs = (qr_ref, kr_ref, vr_ref, gr_ref, qa_ref, ka_ref, va_ref)
    for grp, o_ref in enumerate(outs):
        zg = jnp.dot(hb, w_ref[:, grp * RET_WIDTH:(grp + 1) * RET_WIDTH], preferred_element_type=F32)
        for j in range(RET_WIDTH // LANES):
            z = zg[:, j * LANES:(j + 1) * LANES]
            if grp in (0, 1):
                z = _rotate(z, cr_ref[...], slr_ref[...], shr_ref[...], DK_RET // 2)
                if grp == 1:
                    z = z * (DK_RET ** -0.5)
            elif grp in (4, 5):
                z = _rotate(z, ca_ref[...], sla_ref[...], sha_ref[...], ROT_DIMS // 2)
            o_ref[:, j * LANES:(j + 1) * LANES] = z.astype(o_ref.dtype)
            if t_refs and grp in (5, 6):
                zt = z.T
                for hh in range(HEADS_PER_TILE):
                    t_refs[grp - 5][0, HEADS_PER_TILE * j + hh] = zt[hh * HD_ATT:(hh + 1) * HD_ATT]


def _rot_tables(pos, n_rot, base):
    half = n_rot // 2
    inv = base ** (-jnp.arange(half, dtype=F32) / half)
    ang = pos.astype(F32)[:, None] * inv[None, :]
    cos, sin = jnp.cos(ang), jnp.sin(ang)
    l = np.arange(LANES) % DK_RET
    idx = np.where(l < half, l, np.where(l < n_rot, l - half, 0))
    lo = jnp.asarray(l < half)
    hi = jnp.asarray((l >= half) & (l < n_rot))
    cos_f = jnp.where(jnp.asarray(l < n_rot), cos[:, idx], 1.0)
    s_lo = jnp.where(lo, -sin[:, idx], 0.0)
    s_hi = jnp.where(hi, sin[:, idx], 0.0)
    return cos_f, s_lo, s_hi


def _proj_call(x2d, ln_g, ln_b, w_in_b, tabs, tm, tab_blocks, out_dtypes, seq_for_transposed=None):
    n = x2d.shape[0]
    row = lambda i: (i, 0)
    tab = lambda i: (i % tab_blocks, 0)
    tab_spec = pl.BlockSpec((tm, LANES), tab)
    out_specs = [pl.BlockSpec((tm, RET_WIDTH), row)] * 7
    out_shape = [jax.ShapeDtypeStruct((n, RET_WIDTH), dt) for dt in out_dtypes]
    if seq_for_transposed is not None:
        tps = seq_for_transposed // tm
        out_specs += [pl.BlockSpec((1, H_ATT, HD_ATT, tm), lambda i: (i // tps, 0, 0, i % tps))] * 2
        out_shape += [jax.ShapeDtypeStruct((n // seq_for_transposed, H_ATT, HD_ATT, seq_for_transposed), F32)] * 2
    return pl.pallas_call(
        _proj_kernel,
        grid=(n // tm,),
        in_specs=[pl.BlockSpec((tm, D_MODEL), row), _const_spec((1, D_MODEL)), _const_spec((1, D_MODEL)),
                  _const_spec((D_MODEL, PROJ_WIDTH))] + [tab_spec] * 6,
        out_specs=out_specs,
        out_shape=out_shape,
        compiler_params=pltpu.CompilerParams(dimension_semantics=("parallel",), vmem_limit_bytes=VMEM_LIMIT),
    )(x2d, ln_g, ln_b, w_in_b, *tabs)


COMBINE_ROWS = 256


def _skewed(items, n_stages):
    for t in range(len(items) + n_stages - 1):
        for stage in range(n_stages):
            if 0 <= t - stage < len(items):
                next(items[t - stage], None)
        yield


def _ret_body(q_ref, k_ref, v_ref, g_ref, r_ref, st_ref, hp, n_chunks):
    C = RET_CHUNK
    lane1 = lax.broadcasted_iota(jnp.int32, (1, LANES), 1)
    lg_lane = _select_log_gamma(HEADS_PER_TILE * hp + (lane1 >= DK_RET).astype(jnp.int32))

    ri = lax.broadcasted_iota(jnp.int32, (2 * C, C), 0)
    ci = lax.broadcasted_iota(jnp.int32, (2 * C, C), 1)
    rel = (jnp.where(ri >= C, ri - C, ri) - ci).astype(F32)
    lg_rows = _select_log_gamma(HEADS_PER_TILE * hp + (ri >= C).astype(jnp.int32))
    decay = jnp.where(rel >= 0, jnp.exp(lg_rows * jnp.maximum(rel, 0.0)), 0.0)

    tok = lax.broadcasted_iota(jnp.int32, (C, LANES), 0).astype(F32)
    cross_dec = jnp.exp(lg_lane * (tok + 1.0))
    k_dec = jnp.exp(lg_lane * (C - 1.0 - tok))
    sr = lax.broadcasted_iota(jnp.int32, (LANES, LANES), 0)
    sc = lax.broadcasted_iota(jnp.int32, (LANES, LANES), 1)
    same_head = ((sr >= DK_RET) == (sc >= DK_RET)).astype(F32)
    state_dec = jnp.exp(_select_log_gamma(HEADS_PER_TILE * hp + (sr >= DK_RET).astype(jnp.int32)) * float(C))

    states = [jnp.zeros((LANES, LANES), F32)]

    def chunk(c):
        first = lax.broadcasted_iota(jnp.int32, (1, LANES), 1) < DK_RET
        off = c * C
        q = q_ref[pl.ds(off, C), :]
        k = k_ref[pl.ds(off, C), :]
        v = v_ref[pl.ds(off, C), :]
        zero = jnp.zeros_like(q)
        qs = jnp.concatenate([jnp.where(first, q, zero), jnp.where(first, zero, q)], axis=0)
        s = lax.dot_general(qs, k.astype(BF16), (((1,), (1,)), ((), ())), preferred_element_type=F32)
        upd = lax.dot_general((k * k_dec).astype(BF16), v, (((0,), (0,)), ((), ())),
                              preferred_element_type=F32)
        states.append(state_dec * states[c] + same_head * upd)
        yield
        pv = jnp.dot((s * decay).astype(BF16), v, preferred_element_type=F32)
        cross = jnp.dot(q, states[c].astype(BF16), preferred_element_type=F32) * cross_dec
        yield
        o = jnp.where(first, pv[:C], pv[C:]) + cross
        yield
        s_a = jnp.sum(jnp.where(first, o, 0.0), axis=-1, keepdims=True)
        s_b = jnp.sum(jnp.where(first, 0.0, o), axis=-1, keepdims=True)
        xc = o - jnp.where(first, s_a, s_b) * (1.0 / DK_RET)
        sq = xc * xc
        yield
        v_a = jnp.sum(jnp.where(first, sq, 0.0), axis=-1, keepdims=True)
        v_b = jnp.sum(jnp.where(first, 0.0, sq), axis=-1, keepdims=True)
        rn = xc * lax.rsqrt(jnp.where(first, v_a, v_b) * (1.0 / DK_RET) + LN_EPS)
        g = g_ref[pl.ds(off, C), :]
        r_ref[pl.ds(off, C), :] = (rn * (g * _sigmoid(g))).astype(r_ref.dtype)

    yield from _skewed([chunk(c) for c in range(n_chunks)], 5)
    st_ref[0, 0] = states[n_chunks][:DK_RET, :DK_RET]
    st_ref[0, 1] = states[n_chunks][DK_RET:, DK_RET:]


def _needs_pitch(dil):
    return dil % 8 == 0


def _att_body(q_ref, k_ref, v_ref, o_ref, num_ref, den_ref, max_ref, qp_ref, kp_ref, vp_ref, seq):
    QB = Q_BLOCK
    pitched = [dil for _, dil in DILATIONS if _needs_pitch(dil)]
    assert len(pitched) <= 1
    for dil in pitched:
        for g in range(seq // dil):
            for src, dst in ((q_ref, qp_ref), (k_ref, kp_ref), (v_ref, vp_ref)):
                dst[g * (dil + 1):g * (dil + 1) + dil, :] = src[g * dil:(g + 1) * dil, :]
    ri = lax.broadcasted_iota(jnp.int32, (2 * QB, 2 * QB), 0)
    ci = lax.broadcasted_iota(jnp.int32, (2 * QB, 2 * QB), 1)
    qi = jnp.where(ri >= QB, ri - QB, ri)
    valid_band = ((ci < QB) & (ci >= qi)) | ((ci >= QB) & (ci - QB <= qi))
    bias_band = jnp.where(valid_band, 0.0, NEG_BIG)
    rd = lax.broadcasted_iota(jnp.int32, (2 * QB, QB), 0)
    cd = lax.broadcasted_iota(jnp.int32, (2 * QB, QB), 1)
    bias_diag = jnp.where(cd <= jnp.where(rd >= QB, rd - QB, rd), 0.0, NEG_BIG)
    ones = jnp.ones((2 * QB, LANES), BF16)

    def unit(br, dil, q_start, k_start, n_keys, bias):
        lane = lax.broadcasted_iota(jnp.int32, (1, LANES), 1)
        first = lane < HD_ATT
        rows_q = pl.ds(q_start, QB, stride=dil) if dil > 1 else pl.ds(q_start, QB)
        if _needs_pitch(dil):
            src_q, src_k, src_v = qp_ref, kp_ref, vp_ref
            rows_qs = pl.ds(q_start, QB, stride=dil + 1)
            rows_k = pl.ds(k_start, n_keys, stride=dil + 1)
        else:
            src_q, src_k, src_v = q_ref, k_ref, v_ref
            rows_qs = rows_q
            rows_k = pl.ds(k_start, n_keys, stride=dil) if dil > 1 else pl.ds(k_start, n_keys)
        q = src_q[rows_qs, :] * ATT_SCALE
        qs = jnp.concatenate([jnp.where(first, q, 0.0), jnp.where(first, 0.0, q)], axis=0).astype(BF16)
        kk = src_k[rows_k, :].astype(BF16)
        s = lax.dot_general(qs, kk, (((1,), (1,)), ((), ())), preferred_element_type=F32)
        yield
        s = s + bias
        m = jnp.max(s, axis=-1, keepdims=True)
        yield
        p = jnp.exp(s - m).astype(BF16)
        yield
        vv = src_v[rows_k, :].astype(BF16)
        pv = jnp.dot(p, jnp.concatenate([vv, ones[:n_keys]], axis=1), preferred_element_type=F32)
        yield
        mb = jnp.broadcast_to(m, (2 * QB, LANES))
        num_ref[br, rows_q, :] = jnp.where(first, pv[:QB, :LANES], pv[QB:, :LANES])
        den_ref[br, rows_q, :] = jnp.where(first, pv[:QB, LANES:], pv[QB:, LANES:])
        max_ref[br, rows_q, :] = jnp.where(first, mb[:QB], mb[QB:])

    units = []
    for br, (window, dil) in enumerate(DILATIONS):
        assert window // dil == QB
        sub_len = seq // dil
        n_blocks = sub_len // QB
        assert not (_needs_pitch(dil) and n_blocks > 1)
        for r in range(dil):
            units.append(unit(br, dil, r, r, QB, bias_diag))
            for n in range(1, n_blocks):
                k_start = r + dil * (n - 1) * QB
                units.append(unit(br, dil, k_start + dil * QB, k_start, 2 * QB, bias_band))
    yield from _skewed(units, 5)

    for i in range(seq // COMBINE_ROWS):
        rows = pl.ds(i * COMBINE_ROWS, COMBINE_ROWS)
        m0, m1, m2 = max_ref[0, rows, :], max_ref[1, rows, :], max_ref[2, rows, :]
        mx = jnp.maximum(jnp.maximum(m0, m1), m2)
        e0, e1, e2 = jnp.exp(m0 - mx), jnp.exp(m1 - mx), jnp.exp(m2 - mx)
        num = e0 * num_ref[0, rows, :] + e1 * num_ref[1, rows, :] + e2 * num_ref[2, rows, :]
        den = e0 * den_ref[0, rows, :] + e1 * den_ref[1, rows, :] + e2 * den_ref[2, rows, :]
        o_ref[rows, :] = (num / den).astype(o_ref.dtype)
        yield


ATT_STEPS_PER_RET_CHUNK = 3


def _mixer_kernel(qr_ref, kr_ref, vr_ref, gr_ref, qa_ref, ka_ref, va_ref, r_ref, st_ref, o_ref,
                  num_ref, den_ref, max_ref, qp_ref, kp_ref, vp_ref, *, seq):
    live = [(_ret_body(qr_ref, kr_ref, vr_ref, gr_ref, r_ref, st_ref, pl.program_id(1), seq // RET_CHUNK), 1),
            (_att_body(qa_ref, ka_ref, va_ref, o_ref, num_ref, den_ref, max_ref, qp_ref, kp_ref, vp_ref, seq),
             ATT_STEPS_PER_RET_CHUNK)]
    while live:
        for entry in list(live):
            gen, steps = entry
            for _ in range(steps):
                if next(gen, live) is live:
                    live.remove(entry)
                    break


def _mixer_call(qr, kr, vr, gr, qa, ka, va, batch, seq):
    assert H_RET == H_ATT and seq % RET_CHUNK == 0 and seq % COMBINE_ROWS == 0
    blk = pl.BlockSpec((seq, LANES), lambda b, p: (b, p))
    n_br = len(DILATIONS)
    pitched_rows = max([seq // dil * (dil + 1) for _, dil in DILATIONS if _needs_pitch(dil)] + [8])
    return pl.pallas_call(
        functools.partial(_mixer_kernel, seq=seq),
        grid=(batch, H_RET // HEADS_PER_TILE),
        in_specs=[blk] * 7,
        out_specs=[blk, pl.BlockSpec((1, HEADS_PER_TILE, DK_RET, DK_RET), lambda b, p: (b, p, 0, 0)), blk],
        out_shape=[jax.ShapeDtypeStruct((batch * seq, RET_WIDTH), BF16),
                   jax.ShapeDtypeStruct((batch, H_RET, DK_RET, DK_RET), F32),
                   jax.ShapeDtypeStruct((batch * seq, ATT_WIDTH), BF16)],
        scratch_shapes=[pltpu.VMEM((n_br, seq, LANES), F32)] * 3 + [pltpu.VMEM((pitched_rows, LANES), F32)] * 3,
        compiler_params=pltpu.CompilerParams(dimension_semantics=("parallel", "parallel"),
                                             vmem_limit_bytes=VMEM_LIMIT),
    )(qr, kr, vr, gr, qa, ka, va)


def _post_kernel(*refs, tm, tiles_per_seq, shift):
    (x_ref, r_ref, a_ref, p_ref, lng_ref, lnb_ref, wo_ref, g1_ref, b1_ref, wup_ref, cw_ref, cb_ref,
     wdn_ref, wpg_ref, wpp_ref, g2_ref, b2_ref) = refs[:17]
    if shift:
        y_ref, cs_ref, h1b_s, halo_s = refs[17:]
    else:
        pre0_ref, pre1_ref, y_ref, cs_ref, h1b_s = refs[17:]

    h = _layer_norm(x_ref[...], lng_ref[...], lnb_ref[...])
    mix = (jnp.dot(r_ref[...].astype(BF16), wo_ref[:RET_WIDTH, :], preferred_element_type=F32)
           + jnp.dot(a_ref[...].astype(BF16), wo_ref[RET_WIDTH:, :], preferred_element_type=F32))
    h1 = _layer_norm(ALPHA * h + mix, g1_ref[...], b1_ref[...])
    h1b_s[...] = h1.astype(BF16)
    gate = _sigmoid(jnp.dot(h1b_s[...], wpg_ref[...], preferred_element_type=F32))
    y_ref[...] = ALPHA * h1 + gate * jnp.dot(p_ref[...].astype(BF16), wpp_ref[...], preferred_element_type=F32)

    if shift:
        @pl.when(pl.program_id(0) % tiles_per_seq == 0)
        def _():
            halo_s[...] = jnp.zeros_like(halo_s)
        row8 = lax.broadcasted_iota(jnp.int32, (8, FF_CHUNK), 0)

    n_chunks = D_FF // FF_CHUNK
    partials, acts = [], []
    cols_of = lambda part, j: slice(part * D_FF + j * FF_CHUNK, part * D_FF + (j + 1) * FF_CHUNK)
    kw = D_MODEL // UP_K_SPLIT
    grp = min(tm, ACT_GROUP_ROWS)
    u_of, act_of = {}, {}

    def up_gen(j):
        for part in (0, 1):
            acc = None
            for kq in range(UP_K_SPLIT):
                d = jnp.dot(h1b_s[:, kq * kw:(kq + 1) * kw], wup_ref[kq * kw:(kq + 1) * kw, cols_of(part, j)],
                            preferred_element_type=F32)
                acc = d if acc is None else acc + d
                yield
            u_of[j, part] = acc

    def act_gen(j):
        pieces = []
        for r0 in range(0, tm, grp):
            conv = []
            for part in (0, 1):
                cols = cols_of(part, j)
                u = u_of[j, part][r0:r0 + grp]
                if shift:
                    above = halo_s[:, cols] if r0 == 0 else u_of[j, part][r0 - 8:r0]

                    def shifted(k, u=u, above=above):
                        rolled = pltpu.roll(u, k, 0)
                        top = jnp.where(row8 < k, pltpu.roll(above, k, 0), rolled[0:8])
                        return jnp.concatenate([top, rolled[8:]], axis=0)

                    prev2, prev1 = shifted(2), shifted(1)
                    if r0 + grp == tm:
                        halo_s[:, cols] = u[grp - 8:, :]
                        cs_ref[0, :, cols] = u[grp - (CONV_W - 1):, :]
                else:
                    cs_ref[r0:r0 + grp, cols] = u
                    prev2 = pre0_ref[r0:r0 + grp, cols]
                    prev1 = pre1_ref[r0:r0 + grp, cols]
                conv.append(cb_ref[:, cols] + cw_ref[0:1, cols] * prev2 + cw_ref[1:2, cols] * prev1
                            + cw_ref[2:3, cols] * u)
                yield
            pieces.append((conv[0] * _gelu_tanh(conv[1])).astype(BF16))
            yield
        act_of[j] = jnp.concatenate(pieces, axis=0)

    for j in range(n_chunks + 1):
        live = ([(up_gen(j), 1)] if j < n_chunks else []) + ([(act_gen(j - 1), ACT_STEPS_PER_UP_STEP)] if j else [])
        while live:
            for entry in list(live):
                for _ in range(entry[1]):
                    if next(entry[0], live) is live:
                        live.remove(entry)
                        break
        if j:
            acts.append(act_of.pop(j - 1))
            if len(acts) == DOWN_GROUP or j == n_chunks:
                r0 = (j - len(acts)) * FF_CHUNK
                partials.append(jnp.dot(jnp.concatenate(acts, axis=1), wdn_ref[r0:j * FF_CHUNK, :],
                                        preferred_element_type=F32))
                acts = []

    y_ref[...] = _layer_norm(y_ref[...] + sum(partials[1:], partials[0]), g2_ref[...], b2_ref[...])


def _post_call(x2d, r2d, a2d, p2d, weights, tm, tiles_per_seq, prefix=None):
    n = x2d.shape[0]
    shift = prefix is None
    two_f = 2 * D_FF
    row = lambda i: (i, 0)
    in_specs = [pl.BlockSpec((tm, D_MODEL), row), pl.BlockSpec((tm, RET_WIDTH), row),
                pl.BlockSpec((tm, ATT_WIDTH), row), pl.BlockSpec((tm, PLE_DIM), row)]
    in_specs += [_const_spec(w.shape) for w in weights]
    args = [x2d, r2d, a2d, p2d, *weights]
    scratch = [pltpu.VMEM((tm, D_MODEL), BF16)]
    if shift:
        out_specs = [pl.BlockSpec((tm, D_MODEL), row),
                     pl.BlockSpec((1, CONV_W - 1, two_f), lambda i: (i // tiles_per_seq, 0, 0))]
        out_shape = [jax.ShapeDtypeStruct((n, D_MODEL), F32),
                     jax.ShapeDtypeStruct((n // (tm * tiles_per_seq), CONV_W - 1, two_f), F32)]
        scratch += [pltpu.VMEM((8, two_f), F32)]
    else:
        in_specs += [pl.BlockSpec((tm, two_f), row)] * 2
        args += list(prefix)
        out_specs = [pl.BlockSpec((tm, D_MODEL), row), pl.BlockSpec((tm, two_f), row)]
        out_shape = [jax.ShapeDtypeStruct((n, D_MODEL), F32), jax.ShapeDtypeStruct((n, two_f), F32)]
    return pl.pallas_call(
        functools.partial(_post_kernel, tm=tm, tiles_per_seq=tiles_per_seq, shift=shift),
        grid=(n // tm,),
        in_specs=in_specs,
        out_specs=out_specs,
        out_shape=out_shape,
        scratch_shapes=scratch,
        compiler_params=pltpu.CompilerParams(dimension_semantics=("arbitrary",), vmem_limit_bytes=VMEM_LIMIT),
    )(*args)


def _sample_mix_kernel(qr_ref, kr_ref, vr_ref, gr_ref, st_ref, qa_ref, ka_ref, qat_ref, vat_ref,
                       kt_ref, vt_ref, r_ref, att_ref, nst_ref, s_s, pc_s):
    hrow = lax.broadcasted_iota(jnp.int32, (H_RET, RET_WIDTH), 0)
    hlane = lax.broadcasted_iota(jnp.int32, (H_RET, RET_WIDTH), 1) // DK_RET
    own = hrow == hlane

    qm = jnp.where(own, qr_ref[0], 0.0)
    km = jnp.where(own, kr_ref[0], 0.0)
    v8 = vr_ref[0]
    g8 = gr_ref[0]
    st = st_ref[0]
    lg8 = _select_log_gamma(lax.broadcasted_iota(jnp.int32, (H_RET, 1), 0))
    cross = jnp.dot(qm.astype(BF16), st.astype(BF16), preferred_element_type=F32) * jnp.exp(lg8)
    qk = jnp.sum(qm * km, axis=-1, keepdims=True)
    o = qk * v8 + cross
    lg_rows = _select_log_gamma(lax.broadcasted_iota(jnp.int32, (H_RET * DK_RET, 1), 0) // DK_RET)
    outer = lax.dot_general(km, v8, (((0,), (0,)), ((), ())), preferred_element_type=F32,
                            precision=lax.Precision.HIGHEST)
    nst_ref[0] = jnp.exp(lg_rows) * st + outer
    mu = jnp.mean(o, axis=-1, keepdims=True)
    xc = o - mu
    var = jnp.mean(xc * xc, axis=-1, keepdims=True)
    r_ref[0] = xc * lax.rsqrt(var + LN_EPS) * (g8 * _sigmoid(g8))

    n_past = kt_ref.shape[-1]
    s_new = jnp.sum(qa_ref[0] * ka_ref[0], axis=-1, keepdims=True) * ATT_SCALE
    qt = qat_ref[0] * ATT_SCALE
    for h in range(H_ATT):
        s_s[h:h + 1, :] = jnp.sum(kt_ref[h] * qt[:, h:h + 1], axis=0, keepdims=True)
    tok = lax.broadcasted_iota(jnp.int32, (H_ATT, n_past), 1)
    stats = []
    for window, dil in DILATIONS:
        lo = n_past - window
        sb = s_s[:, lo:] + jnp.where(tok[:, lo:] % dil == n_past % dil, 0.0, NEG_BIG)
        m = jnp.maximum(jnp.max(sb, axis=-1, keepdims=True), s_new)
        p = jnp.exp(sb - m)
        p_new = jnp.exp(s_new - m)
        den = jnp.sum(p, axis=-1, keepdims=True) + p_new
        stats.append((lo, p, p_new, den, m + jnp.log(den)))
    mx = jnp.maximum(jnp.maximum(stats[0][4], stats[1][4]), stats[2][4])
    es = [jnp.exp(st_[4] - mx) for st_ in stats]
    tot = es[0] + es[1] + es[2]
    coef = [e / (tot * st_[3]) for e, st_ in zip(es, stats)]
    w_new = coef[0] * stats[0][2] + coef[1] * stats[1][2] + coef[2] * stats[2][2]
    order = sorted(range(len(stats)), key=lambda i: stats[i][0])
    assert stats[order[0]][0] == 0
    pc_s[...] = coef[order[0]] * stats[order[0]][1]
    for i in order[1:]:
        lo = stats[i][0]
        pc_s[:, lo:] = pc_s[:, lo:] + coef[i] * stats[i][1]
    vt_new = vat_ref[0]
    for h in range(H_ATT):
        col = jnp.sum(vt_ref[h] * pc_s[h:h + 1, :], axis=-1, keepdims=True)
        att_ref[0, :, h:h + 1] = col + w_new[h:h + 1, :] * vt_new[:, h:h + 1]


def _sample_mix_call(qr, kr, vr, gr, state, qa, ka, va, cache_k, cache_v):
    nb = qr.shape[0]
    n_past = cache_k.shape[-1]
    assert all(w <= n_past and n_past % d == 0 for w, d in DILATIONS) and max(w for w, _ in DILATIONS) == n_past
    row3 = pl.BlockSpec((1, 1, RET_WIDTH), lambda b: (b, 0, 0))
    head3 = pl.BlockSpec((1, H_RET, DK_RET), lambda b: (b, 0, 0))
    col3 = pl.BlockSpec((1, HD_ATT, H_ATT), lambda b: (b, 0, 0))
    st_spec = pl.BlockSpec((1, H_RET * DK_RET, DK_RET), lambda b: (b, 0, 0))
    cache_spec = pl.BlockSpec((None, H_ATT, HD_ATT, n_past), lambda b: (b, 0, 0, 0))
    r3 = lambda a: a.reshape(nb, 1, RET_WIDTH)
    h3 = lambda a: a.reshape(nb, H_RET, DK_RET)
    t3 = lambda a: jnp.swapaxes(h3(a), 1, 2)
    r_s, att_t, nst = pl.pallas_call(
        _sample_mix_kernel,
        grid=(nb,),
        in_specs=[row3, row3, head3, head3, st_spec, head3, head3, col3, col3, cache_spec, cache_spec],
        out_specs=[head3, col3, st_spec],
        out_shape=[jax.ShapeDtypeStruct((nb, H_RET, DK_RET), F32),
                   jax.ShapeDtypeStruct((nb, HD_ATT, H_ATT), F32),
                   jax.ShapeDtypeStruct((nb, H_RET * DK_RET, DK_RET), F32)],
        scratch_shapes=[pltpu.VMEM((H_ATT, n_past), F32), pltpu.VMEM((H_ATT, n_past), F32)],
        compiler_params=pltpu.CompilerParams(dimension_semantics=("parallel",), vmem_limit_bytes=VMEM_LIMIT),
    )(r3(qr), r3(kr), h3(vr), h3(gr), state.reshape(nb, H_RET * DK_RET, DK_RET), h3(qa), h3(ka), t3(qa), t3(va),
      cache_k, cache_v)
    return r_s, jnp.swapaxes(att_t, 1, 2), nst


def kernel(x_prompt, x_sample, cache_k_win, cache_v_win, state_ret, state_conv, p_prompt, p_sample,
           ln_in_g, ln_in_b, w_in, w_out, ln1_g, ln1_b, w_up, conv_w, conv_b, w_down,
           w_ple_gate, w_ple_proj, ln2_g, ln2_b):
    B, S, _ = x_prompt.shape
    NB, T, _ = x_sample.shape
    assert T == 1 and w_in.shape[0] == DEPTH == 1 and S % PROJ_TILE == 0 and S % POST_TILE == 0
    two_f = 2 * D_FF
    vec = lambda a: a.reshape(1, -1)
    w_in_b = w_in[0].astype(BF16)
    post_w = (vec(ln_in_g), vec(ln_in_b), w_out[0].astype(BF16), vec(ln1_g[0]), vec(ln1_b[0]),
              w_up[0].astype(BF16), conv_w[0], vec(conv_b[0]), w_down[0].astype(BF16),
              w_ple_gate[0].astype(BF16), w_ple_proj[0].astype(BF16), vec(ln2_g[0]), vec(ln2_b[0]))

    pos_p = jnp.arange(S, dtype=jnp.int32)
    tabs_p = _rot_tables(pos_p, DK_RET, RET_ROPE_BASE) + _rot_tables(pos_p, ROT_DIMS, ATT_ROPE_THETA)
    xp = x_prompt.reshape(B * S, D_MODEL)
    qr, kr, vr, gr, qa, ka, va, ka_t, va_t = _proj_call(
        xp, vec(ln_in_g), vec(ln_in_b), w_in_b, tabs_p, PROJ_TILE, S // PROJ_TILE,
        (BF16, F32, BF16, F32, F32, F32, F32), seq_for_transposed=S)
    r_p, ret_fin, att_p = _mixer_call(qr, kr, vr, gr, qa, ka, va, B, S)
    y_p, conv_p = _post_call(xp, r_p, att_p, p_prompt[0].reshape(B * S, PLE_DIM), post_w,
                             POST_TILE, S // POST_TILE)
    keep = min(WINDOW_MAX, S)
    k_win_p = jnp.transpose(ka_t, (0, 3, 1, 2))[:, S - keep:]
    v_win_p = jnp.transpose(va_t, (0, 3, 1, 2))[:, S - keep:]

    pos_s = jnp.full((NB,), PAST_LEN, jnp.int32)
    tabs_s = _rot_tables(pos_s, DK_RET, RET_ROPE_BASE) + _rot_tables(pos_s, ROT_DIMS, ATT_ROPE_THETA)
    xs = x_sample.reshape(NB, D_MODEL)
    sqr, skr, svr, sgr, sqa, ska, sva = _proj_call(xs, vec(ln_in_g), vec(ln_in_b), w_in_b, tabs_s, NB, 1,
                                                   (F32,) * 7)
    r_s, att_s, nst = _sample_mix_call(sqr, skr, svr, sgr, state_ret[0], sqa, ska, sva,
                                       jnp.transpose(cache_k_win[0], (0, 2, 3, 1)),
                                       jnp.transpose(cache_v_win[0], (0, 2, 3, 1)))
    y_s, u_s = _post_call(xs, r_s.reshape(NB, RET_WIDTH), att_s.reshape(NB, ATT_WIDTH), p_sample[0].reshape(NB, PLE_DIM),
                          post_w, NB, 1, prefix=(state_conv[0][:, 0], state_conv[0][:, 1]))
    conv_s = jnp.stack([state_conv[0][:, 1], u_s], axis=1)

    return (y_p.reshape(B, S, D_MODEL), y_s.reshape(NB, 1, D_MODEL),
            k_win_p[None], v_win_p[None], ret_fin[None], conv_p[None],
            ska.reshape(1, NB, 1, H_ATT, HD_ATT), sva.reshape(1, NB, 1, H_ATT, HD_ATT),
            nst.reshape(1, NB, H_RET, DK_RET, DK_RET), conv_s[None])
```

```python
import functools
import math

import numpy as np
import jax
import jax.numpy as jnp
from jax import lax
from jax.experimental import pallas as pl
from jax.experimental.pallas import tpu as pltpu

F32 = jnp.float32
BF16 = jnp.bfloat16

D_MODEL = 1024
PAST_LEN = 16384
H_RET = 8
DK_RET = 64
RET_WIDTH = 512
RET_CHUNK = 128
RET_ROPE_BASE = 10000.0
H_ATT = 8
HD_ATT = 64
ATT_WIDTH = 512
ATT_ROPE_THETA = 500000.0
ROT_DIMS = HD_ATT // 4
DILATIONS = ((128, 1), (512, 4), (2048, 16))
WINDOW_MAX = 2048
Q_BLOCK = 128
PROJ_WIDTH = 4 * RET_WIDTH + 3 * ATT_WIDTH
D_FF = 2816
CONV_W = 3
PLE_DIM = 256
LN_EPS = 1e-5
DEPTH = 1
ALPHA = (2 * DEPTH) ** 0.25

LANES = 128
HEADS_PER_TILE = LANES // DK_RET
NEG_BIG = -1e30
LOG_GAMMA = tuple(math.log(1.0 - 2.0 ** (-5.0 - h)) for h in range(H_RET))
ATT_SCALE = HD_ATT ** -0.5
VMEM_LIMIT = 56 * 1024 * 1024

PROJ_TILE = 512
POST_TILE = 512
FF_CHUNK = 256
DOWN_GROUP = 3
UP_K_SPLIT = 2
ACT_GROUP_ROWS = 128
ACT_STEPS_PER_UP_STEP = 3


def _layer_norm(x, g, b):
    mu = jnp.mean(x, axis=-1, keepdims=True)
    xc = x - mu
    var = jnp.mean(xc * xc, axis=-1, keepdims=True)
    return xc * lax.rsqrt(var + LN_EPS) * g + b


def _sigmoid(x):
    return 1.0 / (1.0 + jnp.exp(-x))


def _gelu_tanh(x):
    return 0.5 * x * (1.0 + jnp.tanh(math.sqrt(2.0 / math.pi) * (x + 0.044715 * (x * x * x))))


def _select_log_gamma(head_idx):
    out = jnp.zeros(head_idx.shape, F32)
    for h in range(H_RET):
        out = jnp.where(head_idx == h, LOG_GAMMA[h], out)
    return out


def _const_spec(shape):
    nd = len(shape)
    return pl.BlockSpec(shape, lambda *_: (0,) * nd, pipeline_mode=pl.Buffered(1))


def _rotate(z, cos, s_lo, s_hi, half):
    up = pltpu.roll(z, LANES - half, 1)
    dn = pltpu.roll(z, half, 1)
    return z * cos + up * s_lo + dn * s_hi


def _proj_kernel(x_ref, g_ref, b_ref, w_ref, cr_ref, slr_ref, shr_ref, ca_ref, sla_ref, sha_ref,
                 qr_ref, kr_ref, vr_ref, gr_ref, qa_ref, ka_ref, va_ref, *t_refs):
    hb = _layer_norm(x_ref[...], g_ref[...], b_ref[...]).astype(BF16)
    outs = (qr_ref, kr_ref, vr_ref, gr_ref, qa_ref, ka_ref, va_ref)
    for grp, o_ref in enumerate(outs):
        zg = jnp.dot(hb, w_ref[:, grp * RET_WIDTH:(grp + 1) * RET_WIDTH], preferred_element_type=F32)
        for j in range(RET_WIDTH // LANES):
            z = zg[:, j * LANES:(j + 1) * LANES]
            if grp in (0, 1):
                z = _rotate(z, cr_ref[...], slr_ref[...], shr_ref[...], DK_RET // 2)
                if grp == 1:
                    z = z * (DK_RET ** -0.5)
            elif grp in (4, 5):
                z = _rotate(z, ca_ref[...], sla_ref[...], sha_ref[...], ROT_DIMS // 2)
            o_ref[:, j * LANES:(j + 1) * LANES] = z.astype(o_ref.dtype)
            if t_refs and grp in (5, 6):
                zt = z.T
                for hh in range(HEADS_PER_TILE):
                    t_refs[grp - 5][0, HEADS_PER_TILE * j + hh] = zt[hh * HD_ATT:(hh + 1) * HD_ATT]


def _rot_tables(pos, n_rot, base):
    half = n_rot // 2
    inv = base ** (-jnp.arange(half, dtype=F32) / half)
    ang = pos.astype(F32)[:, None] * inv[None, :]
    cos, sin = jnp.cos(ang), jnp.sin(ang)
    l = np.arange(LANES) % DK_RET
    idx = np.where(l < half, l, np.where(l < n_rot, l - half, 0))
    lo = jnp.asarray(l < half)
    hi = jnp.asarray((l >= half) & (l < n_rot))
    cos_f = jnp.where(jnp.asarray(l < n_rot), cos[:, idx], 1.0)
    s_lo = jnp.where(lo, -sin[:, idx], 0.0)
    s_hi = jnp.where(hi, sin[:, idx], 0.0)
    return cos_f, s_lo, s_hi


def _proj_call(x2d, ln_g, ln_b, w_in_b, tabs, tm, tab_blocks, out_dtypes, seq_for_transposed=None):
    n = x2d.shape[0]
    row = lambda i: (i, 0)
    tab = lambda i: (i % tab_blocks, 0)
    tab_spec = pl.BlockSpec((tm, LANES), tab)
    out_specs = [pl.BlockSpec((tm, RET_WIDTH), row)] * 7
    out_shape = [jax.ShapeDtypeStruct((n, RET_WIDTH), dt) for dt in out_dtypes]
    if seq_for_transposed is not None:
        tps = seq_for_transposed // tm
        out_specs += [pl.BlockSpec((1, H_ATT, HD_ATT, tm), lambda i: (i // tps, 0, 0, i % tps))] * 2
        out_shape += [jax.ShapeDtypeStruct((n // seq_for_transposed, H_ATT, HD_ATT, seq_for_transposed), F32)] * 2
    return pl.pallas_call(
        _proj_kernel,
        grid=(n // tm,),
        in_specs=[pl.BlockSpec((tm, D_MODEL), row), _const_spec((1, D_MODEL)), _const_spec((1, D_MODEL)),
                  _const_spec((D_MODEL, PROJ_WIDTH))] + [tab_spec] * 6,
        out_specs=out_specs,
        out_shape=out_shape,
        compiler_params=pltpu.CompilerParams(dimension_semantics=("parallel",), vmem_limit_bytes=VMEM_LIMIT),
    )(x2d, ln_g, ln_b, w_in_b, *tabs)


COMBINE_ROWS = 256


def _round_robin(entries):
    live = list(entries)
    while live:
        for entry in list(live):
            for _ in range(entry[1]):
                if next(entry[0], live) is live:
                    live.remove(entry)
                    break
                yield


def _skewed(items, n_stages):
    for t in range(len(items) + n_stages - 1):
        for stage in range(n_stages):
            if 0 <= t - stage < len(items):
                next(items[t - stage], None)
        yield


def _ret_body(q_ref, k_ref, v_ref, g_ref, r_ref, st_ref, hp, n_chunks):
    C = RET_CHUNK
    lane1 = lax.broadcasted_iota(jnp.int32, (1, LANES), 1)
    lg_lane = _select_log_gamma(HEADS_PER_TILE * hp + (lane1 >= DK_RET).astype(jnp.int32))

    ri = lax.broadcasted_iota(jnp.int32, (2 * C, C), 0)
    ci = lax.broadcasted_iota(jnp.int32, (2 * C, C), 1)
    rel = (jnp.where(ri >= C, ri - C, ri) - ci).astype(F32)
    lg_rows = _select_log_gamma(HEADS_PER_TILE * hp + (ri >= C).astype(jnp.int32))
    decay = jnp.where(rel >= 0, jnp.exp(lg_rows * jnp.maximum(rel, 0.0)), 0.0)

    tok = lax.broadcasted_iota(jnp.int32, (C, LANES), 0).astype(F32)
    cross_dec = jnp.exp(lg_lane * (tok + 1.0))
    k_dec = jnp.exp(lg_lane * (C - 1.0 - tok))
    sr = lax.broadcasted_iota(jnp.int32, (LANES, LANES), 0)
    sc = lax.broadcasted_iota(jnp.int32, (LANES, LANES), 1)
    same_head = ((sr >= DK_RET) == (sc >= DK_RET)).astype(F32)
    state_dec = jnp.exp(_select_log_gamma(HEADS_PER_TILE * hp + (sr >= DK_RET).astype(jnp.int32)) * float(C))

    states = [jnp.zeros((LANES, LANES), F32)]

    def chunk(c):
        first = lax.broadcasted_iota(jnp.int32, (1, LANES), 1) < DK_RET
        off = c * C
        q = q_ref[pl.ds(off, C), :]
        k = k_ref[pl.ds(off, C), :]
        v = v_ref[pl.ds(off, C), :]
        zero = jnp.zeros_like(q)
        qs = jnp.concatenate([jnp.where(first, q, zero), jnp.where(first, zero, q)], axis=0)
        s = lax.dot_general(qs, k.astype(BF16), (((1,), (1,)), ((), ())), preferred_element_type=F32)
        upd = lax.dot_general((k * k_dec).astype(BF16), v, (((0,), (0,)), ((), ())),
                              preferred_element_type=F32)
        states.append(state_dec * states[c] + same_head * upd)
        yield
        pv = jnp.dot((s * decay).astype(BF16), v, preferred_element_type=F32)
        cross = jnp.dot(q, states[c].astype(BF16), preferred_element_type=F32) * cross_dec
        yield
        o = jnp.where(first, pv[:C], pv[C:]) + cross
        yield
        s_a = jnp.sum(jnp.where(first, o, 0.0), axis=-1, keepdims=True)
        s_b = jnp.sum(jnp.where(first, 0.0, o), axis=-1, keepdims=True)
        xc = o - jnp.where(first, s_a, s_b) * (1.0 / DK_RET)
        sq = xc * xc
        yield
        v_a = jnp.sum(jnp.where(first, sq, 0.0), axis=-1, keepdims=True)
        v_b = jnp.sum(jnp.where(first, 0.0, sq), axis=-1, keepdims=True)
        rn = xc * lax.rsqrt(jnp.where(first, v_a, v_b) * (1.0 / DK_RET) + LN_EPS)
        g = g_ref[pl.ds(off, C), :]
        r_ref[pl.ds(off, C), :] = (rn * (g * _sigmoid(g))).astype(r_ref.dtype)

    yield from _skewed([chunk(c) for c in range(n_chunks)], 5)
    st_ref[0, 0] = states[n_chunks][:DK_RET, :DK_RET]
    st_ref[0, 1] = states[n_chunks][DK_RET:, DK_RET:]


def _needs_pitch(dil):
    return dil % 8 == 0


def _att_body(q_ref, k_ref, v_ref, o_ref, num_ref, den_ref, max_ref, qp_ref, kp_ref, vp_ref, seq):
    QB = Q_BLOCK
    pitched = [dil for _, dil in DILATIONS if _needs_pitch(dil)]
    assert len(pitched) <= 1
    for dil in pitched:
        for g in range(seq // dil):
            for src, dst in ((q_ref, qp_ref), (k_ref, kp_ref), (v_ref, vp_ref)):
                dst[g * (dil + 1):g * (dil + 1) + dil, :] = src[g * dil:(g + 1) * dil, :]
    ri = lax.broadcasted_iota(jnp.int32, (2 * QB, 2 * QB), 0)
    ci = lax.broadcasted_iota(jnp.int32, (2 * QB, 2 * QB), 1)
    qi = jnp.where(ri >= QB, ri - QB, ri)
    valid_band = ((ci < QB) & (ci >= qi)) | ((ci >= QB) & (ci - QB <= qi))
    bias_band = jnp.where(valid_band, 0.0, NEG_BIG)
    rd = lax.broadcasted_iota(jnp.int32, (2 * QB, QB), 0)
    cd = lax.broadcasted_iota(jnp.int32, (2 * QB, QB), 1)
    bias_diag = jnp.where(cd <= jnp.where(rd >= QB, rd - QB, rd), 0.0, NEG_BIG)
    ones = jnp.ones((2 * QB, LANES), BF16)

    def unit(br, dil, q_start, k_start, n_keys, bias):
        lane = lax.broadcasted_iota(jnp.int32, (1, LANES), 1)
        first = lane < HD_ATT
        rows_q = pl.ds(q_start, QB, stride=dil) if dil > 1 else pl.ds(q_start, QB)
        if _needs_pitch(dil):
            src_q, src_k, src_v = qp_ref, kp_ref, vp_ref
            rows_qs = pl.ds(q_start, QB, stride=dil + 1)
            rows_k = pl.ds(k_start, n_keys, stride=dil + 1)
        else:
            src_q, src_k, src_v = q_ref, k_ref, v_ref
            rows_qs = rows_q
            rows_k = pl.ds(k_start, n_keys, stride=dil) if dil > 1 else pl.ds(k_start, n_keys)
        q = src_q[rows_qs, :] * ATT_SCALE
        qs = jnp.concatenate([jnp.where(first, q, 0.0), jnp.where(first, 0.0, q)], axis=0).astype(BF16)
        kk = src_k[rows_k, :].astype(BF16)
        s = lax.dot_general(qs, kk, (((1,), (1,)), ((), ())), preferred_element_type=F32)
        yield
        s = s + bias
        m = jnp.max(s, axis=-1, keepdims=True)
        yield
        p = jnp.exp(s - m).astype(BF16)
        yield
        vv = src_v[rows_k, :].astype(BF16)
        pv = jnp.dot(p, jnp.concatenate([vv, ones[:n_keys]], axis=1), preferred_element_type=F32)
        yield
        mb = jnp.broadcast_to(m, (2 * QB, LANES))
        num_ref[br, rows_q, :] = jnp.where(first, pv[:QB, :LANES], pv[QB:, :LANES])
        den_ref[br, rows_q, :] = jnp.where(first, pv[:QB, LANES:], pv[QB:, LANES:])
        max_ref[br, rows_q, :] = jnp.where(first, mb[:QB], mb[QB:])

    units = []
    for br, (window, dil) in enumerate(DILATIONS):
        assert window // dil == QB
        sub_len = seq // dil
        n_blocks = sub_len // QB
        assert not (_needs_pitch(dil) and n_blocks > 1)
        for r in range(dil):
            units.append(unit(br, dil, r, r, QB, bias_diag))
            for n in range(1, n_blocks):
                k_start = r + dil * (n - 1) * QB
                units.append(unit(br, dil, k_start + dil * QB, k_start, 2 * QB, bias_band))
    yield from _skewed(units, 5)

    for i in range(seq // COMBINE_ROWS):
        rows = pl.ds(i * COMBINE_ROWS, COMBINE_ROWS)
        m0, m1, m2 = max_ref[0, rows, :], max_ref[1, rows, :], max_ref[2, rows, :]
        mx = jnp.maximum(jnp.maximum(m0, m1), m2)
        e0, e1, e2 = jnp.exp(m0 - mx), jnp.exp(m1 - mx), jnp.exp(m2 - mx)
        num = e0 * num_ref[0, rows, :] + e1 * num_ref[1, rows, :] + e2 * num_ref[2, rows, :]
        den = e0 * den_ref[0, rows, :] + e1 * den_ref[1, rows, :] + e2 * den_ref[2, rows, :]
        o_ref[rows, :] = (num / den).astype(o_ref.dtype)
        yield


ATT_STEPS_PER_RET_CHUNK = 2


def _mixer_kernel(qr_ref, kr_ref, vr_ref, gr_ref, qa_ref, ka_ref, va_ref, r_ref, st_ref, o_ref,
                  num_ref, den_ref, max_ref, qp_ref, kp_ref, vp_ref, *, seq):
    for _ in _round_robin([
            (_ret_body(qr_ref, kr_ref, vr_ref, gr_ref, r_ref, st_ref, pl.program_id(1), seq // RET_CHUNK), 1),
            (_att_body(qa_ref, ka_ref, va_ref, o_ref, num_ref, den_ref, max_ref, qp_ref, kp_ref, vp_ref, seq),
             ATT_STEPS_PER_RET_CHUNK)]):
        pass


def _mixer_call(qr, kr, vr, gr, qa, ka, va, batch, seq):
    assert H_RET == H_ATT and seq % RET_CHUNK == 0 and seq % COMBINE_ROWS == 0
    blk = pl.BlockSpec((seq, LANES), lambda b, p: (b, p))
    n_br = len(DILATIONS)
    pitched_rows = max([seq // dil * (dil + 1) for _, dil in DILATIONS if _needs_pitch(dil)] + [8])
    return pl.pallas_call(
        functools.partial(_mixer_kernel, seq=seq),
        grid=(batch, H_RET // HEADS_PER_TILE),
        in_specs=[blk] * 7,
        out_specs=[blk, pl.BlockSpec((1, HEADS_PER_TILE, DK_RET, DK_RET), lambda b, p: (b, p, 0, 0)), blk],
        out_shape=[jax.ShapeDtypeStruct((batch * seq, RET_WIDTH), BF16),
                   jax.ShapeDtypeStruct((batch, H_RET, DK_RET, DK_RET), F32),
                   jax.ShapeDtypeStruct((batch * seq, ATT_WIDTH), BF16)],
        scratch_shapes=[pltpu.VMEM((n_br, seq, LANES), F32)] * 3 + [pltpu.VMEM((pitched_rows, LANES), F32)] * 3,
        compiler_params=pltpu.CompilerParams(dimension_semantics=("parallel", "parallel"),
                                             vmem_limit_bytes=VMEM_LIMIT),
    )(qr, kr, vr, gr, qa, ka, va)


def _post_kernel(*refs, tm, tiles_per_seq, shift):
    (x_ref, r_ref, a_ref, p_ref, lng_ref, lnb_ref, wo_ref, g1_ref, b1_ref, wup_ref, cw_ref, cb_ref,
     wdn_ref, wpg_ref, wpp_ref, g2_ref, b2_ref) = refs[:17]
    if shift:
        y_ref, cs_ref, h1b_s, halo_s = refs[17:]
    else:
        pre0_ref, pre1_ref, y_ref, cs_ref, h1b_s = refs[17:]

    h = _layer_norm(x_ref[...], lng_ref[...], lnb_ref[...])
    mix = (jnp.dot(r_ref[...].astype(BF16), wo_ref[:RET_WIDTH, :], preferred_element_type=F32)
           + jnp.dot(a_ref[...].astype(BF16), wo_ref[RET_WIDTH:, :], preferred_element_type=F32))
    h1 = _layer_norm(ALPHA * h + mix, g1_ref[...], b1_ref[...])
    h1b_s[...] = h1.astype(BF16)
    gate = _sigmoid(jnp.dot(h1b_s[...], wpg_ref[...], preferred_element_type=F32))
    y_ref[...] = ALPHA * h1 + gate * jnp.dot(p_ref[...].astype(BF16), wpp_ref[...], preferred_element_type=F32)

    if shift:
        @pl.when(pl.program_id(0) % tiles_per_seq == 0)
        def _():
            halo_s[...] = jnp.zeros_like(halo_s)
        row8 = lax.broadcasted_iota(jnp.int32, (8, FF_CHUNK), 0)

    n_chunks = D_FF // FF_CHUNK
    partials, acts = [], []
    cols_of = lambda part, j: slice(part * D_FF + j * FF_CHUNK, part * D_FF + (j + 1) * FF_CHUNK)
    kw = D_MODEL // UP_K_SPLIT
    grp = min(tm, ACT_GROUP_ROWS)
    u_of = {}

    def up_gen(j):
        for part in (0, 1):
            acc = None
            for kq in range(UP_K_SPLIT):
                d = jnp.dot(h1b_s[:, kq * kw:(kq + 1) * kw], wup_ref[kq * kw:(kq + 1) * kw, cols_of(part, j)],
                            preferred_element_type=F32)
                acc = d if acc is None else acc + d
                yield
            u_of[j, part] = acc

    def act_gen(j, out):
        pieces = []
        for r0 in range(0, tm, grp):
            conv = []
            for part in (0, 1):
                cols = cols_of(part, j)
                u = u_of[j, part][r0:r0 + grp]
                if shift:
                    above = halo_s[:, cols] if r0 == 0 else u_of[j, part][r0 - 8:r0]

                    def shifted(k, u=u, above=above):
                        rolled = pltpu.roll(u, k, 0)
                        top = jnp.where(row8 < k, pltpu.roll(above, k, 0), rolled[0:8])
                        return jnp.concatenate([top, rolled[8:]], axis=0)

                    prev2, prev1 = shifted(2), shifted(1)
                    if r0 + grp == tm:
                        halo_s[:, cols] = u[grp - 8:, :]
                        cs_ref[0, :, cols] = u[grp - (CONV_W - 1):, :]
                else:
                    cs_ref[r0:r0 + grp, cols] = u
                    prev2 = pre0_ref[r0:r0 + grp, cols]
                    prev1 = pre1_ref[r0:r0 + grp, cols]
                conv.append(cb_ref[:, cols] + cw_ref[0:1, cols] * prev2 + cw_ref[1:2, cols] * prev1
                            + cw_ref[2:3, cols] * u)
                yield
            pieces.append((conv[0] * _gelu_tanh(conv[1])).astype(BF16))
            yield
        del u_of[j, 0], u_of[j, 1]
        out.append(jnp.concatenate(pieces, axis=0))

    for j in range(n_chunks + 1):
        for _ in _round_robin(([(up_gen(j), 1)] if j < n_chunks else [])
                              + ([(act_gen(j - 1, acts), ACT_STEPS_PER_UP_STEP)] if j else [])):
            pass
        if j and (len(acts) == DOWN_GROUP or j == n_chunks):
            r0 = (j - len(acts)) * FF_CHUNK
            partials.append(jnp.dot(jnp.concatenate(acts, axis=1), wdn_ref[r0:j * FF_CHUNK, :],
                                    preferred_element_type=F32))
            acts = []

    y_ref[...] = _layer_norm(y_ref[...] + sum(partials[1:], partials[0]), g2_ref[...], b2_ref[...])


def _post_call(x2d, r2d, a2d, p2d, weights, tm, tiles_per_seq, prefix=None):
    n = x2d.shape[0]
    shift = prefix is None
    two_f = 2 * D_FF
    row = lambda i: (i, 0)
    in_specs = [pl.BlockSpec((tm, D_MODEL), row), pl.BlockSpec((tm, RET_WIDTH), row),
                pl.BlockSpec((tm, ATT_WIDTH), row), pl.BlockSpec((tm, PLE_DIM), row)]
    in_specs += [_const_spec(w.shape) for w in weights]
    args = [x2d, r2d, a2d, p2d, *weights]
    scratch = [pltpu.VMEM((tm, D_MODEL), BF16)]
    if shift:
        out_specs = [pl.BlockSpec((tm, D_MODEL), row),
                     pl.BlockSpec((1, CONV_W - 1, two_f), lambda i: (i // tiles_per_seq, 0, 0))]
        out_shape = [jax.ShapeDtypeStruct((n, D_MODEL), F32),
                     jax.ShapeDtypeStruct((n // (tm * tiles_per_seq), CONV_W - 1, two_f), F32)]
        scratch += [pltpu.VMEM((8, two_f), F32)]
    else:
        in_specs += [pl.BlockSpec((tm, two_f), row)] * 2
        args += list(prefix)
        out_specs = [pl.BlockSpec((tm, D_MODEL), row), pl.BlockSpec((tm, two_f), row)]
        out_shape = [jax.ShapeDtypeStruct((n, D_MODEL), F32), jax.ShapeDtypeStruct((n, two_f), F32)]
    return pl.pallas_call(
        functools.partial(_post_kernel, tm=tm, tiles_per_seq=tiles_per_seq, shift=shift),
        grid=(n // tm,),
        in_specs=in_specs,
        out_specs=out_specs,
        out_shape=out_shape,
        scratch_shapes=scratch,
        compiler_params=pltpu.CompilerParams(dimension_semantics=("arbitrary",), vmem_limit_bytes=VMEM_LIMIT),
    )(*args)


def _sample_mix_kernel(qr_ref, kr_ref, vr_ref, gr_ref, st_ref, qa_ref, ka_ref, qat_ref, vat_ref,
                       kt_ref, vt_ref, r_ref, att_ref, nst_ref, s_s, pc_s):
    hrow = lax.broadcasted_iota(jnp.int32, (H_RET, RET_WIDTH), 0)
    hlane = lax.broadcasted_iota(jnp.int32, (H_RET, RET_WIDTH), 1) // DK_RET
    own = hrow == hlane

    qm = jnp.where(own, qr_ref[0], 0.0)
    km = jnp.where(own, kr_ref[0], 0.0)
    v8 = vr_ref[0]
    g8 = gr_ref[0]
    st = st_ref[0]
    lg8 = _select_log_gamma(lax.broadcasted_iota(jnp.int32, (H_RET, 1), 0))
    cross = jnp.dot(qm.astype(BF16), st.astype(BF16), preferred_element_type=F32) * jnp.exp(lg8)
    qk = jnp.sum(qm * km, axis=-1, keepdims=True)
    o = qk * v8 + cross
    lg_rows = _select_log_gamma(lax.broadcasted_iota(jnp.int32, (H_RET * DK_RET, 1), 0) // DK_RET)
    outer = lax.dot_general(km, v8, (((0,), (0,)), ((), ())), preferred_element_type=F32,
                            precision=lax.Precision.HIGHEST)
    nst_ref[0] = jnp.exp(lg_rows) * st + outer
    mu = jnp.mean(o, axis=-1, keepdims=True)
    xc = o - mu
    var = jnp.mean(xc * xc, axis=-1, keepdims=True)
    r_ref[0] = xc * lax.rsqrt(var + LN_EPS) * (g8 * _sigmoid(g8))

    n_past = kt_ref.shape[-1]
    s_new = jnp.sum(qa_ref[0] * ka_ref[0], axis=-1, keepdims=True) * ATT_SCALE
    qt = qat_ref[0] * ATT_SCALE
    for h in range(H_ATT):
        s_s[h:h + 1, :] = jnp.sum(kt_ref[h] * qt[:, h:h + 1], axis=0, keepdims=True)
    tok = lax.broadcasted_iota(jnp.int32, (H_ATT, n_past), 1)
    stats = []
    for window, dil in DILATIONS:
        lo = n_past - window
        sb = s_s[:, lo:] + jnp.where(tok[:, lo:] % dil == n_past % dil, 0.0, NEG_BIG)
        m = jnp.maximum(jnp.max(sb, axis=-1, keepdims=True), s_new)
        p = jnp.exp(sb - m)
        p_new = jnp.exp(s_new - m)
        den = jnp.sum(p, axis=-1, keepdims=True) + p_new
        stats.append((lo, p, p_new, den, m + jnp.log(den)))
    mx = jnp.maximum(jnp.maximum(stats[0][4], stats[1][4]), stats[2][4])
    es = [jnp.exp(st_[4] - mx) for st_ in stats]
    tot = es[0] + es[1] + es[2]
    coef = [e / (tot * st_[3]) for e, st_ in zip(es, stats)]
    w_new = coef[0] * stats[0][2] + coef[1] * stats[1][2] + coef[2] * stats[2][2]
    order = sorted(range(len(stats)), key=lambda i: stats[i][0])
    assert stats[order[0]][0] == 0
    pc_s[...] = coef[order[0]] * stats[order[0]][1]
    for i in order[1:]:
        lo = stats[i][0]
        pc_s[:, lo:] = pc_s[:, lo:] + coef[i] * stats[i][1]
    vt_new = vat_ref[0]
    for h in range(H_ATT):
        col = jnp.sum(vt_ref[h] * pc_s[h:h + 1, :], axis=-1, keepdims=True)
        att_ref[0, :, h:h + 1] = col + w_new[h:h + 1, :] * vt_new[:, h:h + 1]


def _sample_mix_call(qr, kr, vr, gr, state, qa, ka, va, cache_k, cache_v):
    nb = qr.shape[0]
    n_past = cache_k.shape[-1]
    assert all(w <= n_past and n_past % d == 0 for w, d in DILATIONS) and max(w for w, _ in DILATIONS) == n_past
    row3 = pl.BlockSpec((1, 1, RET_WIDTH), lambda b: (b, 0, 0))
    head3 = pl.BlockSpec((1, H_RET, DK_RET), lambda b: (b, 0, 0))
    col3 = pl.BlockSpec((1, HD_ATT, H_ATT), lambda b: (b, 0, 0))
    st_spec = pl.BlockSpec((1, H_RET * DK_RET, DK_RET), lambda b: (b, 0, 0))
    cache_spec = pl.BlockSpec((None, H_ATT, HD_ATT, n_past), lambda b: (b, 0, 0, 0))
    r3 = lambda a: a.reshape(nb, 1, RET_WIDTH)
    h3 = lambda a: a.reshape(nb, H_RET, DK_RET)
    t3 = lambda a: jnp.swapaxes(h3(a), 1, 2)
    r_s, att_t, nst = pl.pallas_call(
        _sample_mix_kernel,
        grid=(nb,),
        in_specs=[row3, row3, head3, head3, st_spec, head3, head3, col3, col3, cache_spec, cache_spec],
        out_specs=[head3, col3, st_spec],
        out_shape=[jax.ShapeDtypeStruct((nb, H_RET, DK_RET), F32),
                   jax.ShapeDtypeStruct((nb, HD_ATT, H_ATT), F32),
                   jax.ShapeDtypeStruct((nb, H_RET * DK_RET, DK_RET), F32)],
        scratch_shapes=[pltpu.VMEM((H_ATT, n_past), F32), pltpu.VMEM((H_ATT, n_past), F32)],
        compiler_params=pltpu.CompilerParams(dimension_semantics=("parallel",), vmem_limit_bytes=VMEM_LIMIT),
    )(r3(qr), r3(kr), h3(vr), h3(gr), state.reshape(nb, H_RET * DK_RET, DK_RET), h3(qa), h3(ka), t3(qa), t3(va),
      cache_k, cache_v)
    return r_s, jnp.swapaxes(att_t, 1, 2), nst


def kernel(x_prompt, x_sample, cache_k_win, cache_v_win, state_ret, state_conv, p_prompt, p_sample,
           ln_in_g, ln_in_b, w_in, w_out, ln1_g, ln1_b, w_up, conv_w, conv_b, w_down,
           w_ple_gate, w_ple_proj, ln2_g, ln2_b):
    B, S, _ = x_prompt.shape
    NB, T, _ = x_sample.shape
    assert T == 1 and w_in.shape[0] == DEPTH == 1 and S % PROJ_TILE == 0 and S % POST_TILE == 0
    two_f = 2 * D_FF
    vec = lambda a: a.reshape(1, -1)
    w_in_b = w_in[0].astype(BF16)
    post_w = (vec(ln_in_g), vec(ln_in_b), w_out[0].astype(BF16), vec(ln1_g[0]), vec(ln1_b[0]),
              w_up[0].astype(BF16), conv_w[0], vec(conv_b[0]), w_down[0].astype(BF16),
              w_ple_gate[0].astype(BF16), w_ple_proj[0].astype(BF16), vec(ln2_g[0]), vec(ln2_b[0]))

    pos_p = jnp.arange(S, dtype=jnp.int32)
    tabs_p = _rot_tables(pos_p, DK_RET, RET_ROPE_BASE) + _rot_tables(pos_p, ROT_DIMS, ATT_ROPE_THETA)
    xp = x_prompt.reshape(B * S, D_MODEL)
    qr, kr, vr, gr, qa, ka, va, ka_t, va_t = _proj_call(
        xp, vec(ln_in_g), vec(ln_in_b), w_in_b, tabs_p, PROJ_TILE, S // PROJ_TILE,
        (BF16, F32, BF16, F32, F32, F32, F32), seq_for_transposed=S)
    r_p, ret_fin, att_p = _mixer_call(qr, kr, vr, gr, qa, ka, va, B, S)
    y_p, conv_p = _post_call(xp, r_p, att_p, p_prompt[0].reshape(B * S, PLE_DIM), post_w,
                             POST_TILE, S // POST_TILE)
    keep = min(WINDOW_MAX, S)
    k_win_p = jnp.transpose(ka_t, (0, 3, 1, 2))[:, S - keep:]
    v_win_p = jnp.transpose(va_t, (0, 3, 1, 2))[:, S - keep:]

    pos_s = jnp.full((NB,), PAST_LEN, jnp.int32)
    tabs_s = _rot_tables(pos_s, DK_RET, RET_ROPE_BASE) + _rot_tables(pos_s, ROT_DIMS, ATT_ROPE_THETA)
    xs = x_sample.reshape(NB, D_MODEL)
    sqr, skr, svr, sgr, sqa, ska, sva = _proj_call(xs, vec(ln_in_g), vec(ln_in_b), w_in_b, tabs_s, NB, 1,
                                                   (F32,) * 7)
    r_s, att_s, nst = _sample_mix_call(sqr, skr, svr, sgr, state_ret[0], sqa, ska, sva,
                                       jnp.transpose(cache_k_win[0], (0, 2, 3, 1)),
                                       jnp.transpose(cache_v_win[0], (0, 2, 3, 1)))
    y_s, u_s = _post_call(xs, r_s.reshape(NB, RET_WIDTH), att_s.reshape(NB, ATT_WIDTH), p_sample[0].reshape(NB, PLE_DIM),
                          post_w, NB, 1, prefix=(state_conv[0][:, 0], state_conv[0][:, 1]))
    conv_s = jnp.stack([state_conv[0][:, 1], u_s], axis=1)

    return (y_p.reshape(B, S, D_MODEL), y_s.reshape(NB, 1, D_MODEL),
            k_win_p[None], v_win_p[None], ret_fin[None], conv_p[None],
            ska.reshape(1, NB, 1, H_ATT, HD_ATT), sva.reshape(1, NB, 1, H_ATT, HD_ATT),
            nst.reshape(1, NB, H_RET, DK_RET, DK_RET), conv_s[None])
```

```python
import functools
import math

import numpy as np
import jax
import jax.numpy as jnp
from jax import lax
from jax.experimental import pallas as pl
from jax.experimental.pallas import tpu as pltpu

F32 = jnp.float32
BF16 = jnp.bfloat16

D_MODEL = 1024
PAST_LEN = 16384
H_RET = 8
DK_RET = 64
RET_WIDTH = 512
RET_CHUNK = 128
RET_ROPE_BASE = 10000.0
H_ATT = 8
HD_ATT = 64
ATT_WIDTH = 512
ATT_ROPE_THETA = 500000.0
ROT_DIMS = HD_ATT // 4
DILATIONS = ((128, 1), (512, 4), (2048, 16))
WINDOW_MAX = 2048
Q_BLOCK = 128
PROJ_WIDTH = 4 * RET_WIDTH + 3 * ATT_WIDTH
D_FF = 2816
CONV_W = 3
PLE_DIM = 256
LN_EPS = 1e-5
DEPTH = 1
ALPHA = (2 * DEPTH) ** 0.25

LANES = 128
HEADS_PER_TILE = LANES // DK_RET
NEG_BIG = -1e30
LOG_GAMMA = tuple(math.log(1.0 - 2.0 ** (-5.0 - h)) for h in range(H_RET))
ATT_SCALE = HD_ATT ** -0.5
VMEM_LIMIT = 56 * 1024 * 1024

PROJ_TILE = 512
POST_TILE = 512
FF_CHUNK = 256
DOWN_GROUP = 3
UP_K_SPLIT = 2
ACT_GROUP_ROWS = 128
ACT_STEPS_PER_UP_STEP = 3


def _layer_norm(x, g, b):
    mu = jnp.mean(x, axis=-1, keepdims=True)
    xc = x - mu
    var = jnp.mean(xc * xc, axis=-1, keepdims=True)
    return xc * lax.rsqrt(var + LN_EPS) * g + b


def _sigmoid(x):
    return 1.0 / (1.0 + jnp.exp(-x))


def _gelu_tanh(x):
    return 0.5 * x * (1.0 + jnp.tanh(math.sqrt(2.0 / math.pi) * (x + 0.044715 * (x * x * x))))


def _select_log_gamma(head_idx):
    out = jnp.zeros(head_idx.shape, F32)
    for h in range(H_RET):
        out = jnp.where(head_idx == h, LOG_GAMMA[h], out)
    return out


def _const_spec(shape):
    nd = len(shape)
    return pl.BlockSpec(shape, lambda *_: (0,) * nd, pipeline_mode=pl.Buffered(1))


def _rotate(z, cos, s_lo, s_hi, half):
    up = pltpu.roll(z, LANES - half, 1)
    dn = pltpu.roll(z, half, 1)
    return z * cos + up * s_lo + dn * s_hi


PROJ_GROUP_ORDER = (5, 6, 0, 1, 4, 3, 2)


def _proj_kernel(x_ref, g_ref, b_ref, w_ref, cr_ref, slr_ref, shr_ref, ca_ref, sla_ref, sha_ref,
                 qr_ref, kr_ref, vr_ref, gr_ref, qa_ref, ka_ref, va_ref, *t_refs):
    outs = (qr_ref, kr_ref, vr_ref, gr_ref, qa_ref, ka_ref, va_ref)
    tm = x_ref.shape[0]
    cols = lambda grp: slice(grp * RET_WIDTH, (grp + 1) * RET_WIDTH)
    part = tm // 2 if tm % 32 == 0 else tm
    hb_parts = [_layer_norm(x_ref[r0:r0 + part], g_ref[...], b_ref[...]).astype(BF16) for r0 in range(0, tm, part)]
    z_next = jnp.concatenate([jnp.dot(hp, w_ref[:, cols(PROJ_GROUP_ORDER[0])], preferred_element_type=F32)
                              for hp in hb_parts], axis=0)
    hb = jnp.concatenate(hb_parts, axis=0)
    for i, grp in enumerate(PROJ_GROUP_ORDER):
        o_ref = outs[grp]
        zg = z_next
        if i + 1 < len(PROJ_GROUP_ORDER):
            z_next = jnp.dot(hb, w_ref[:, cols(PROJ_GROUP_ORDER[i + 1])], preferred_element_type=F32)
        for j in range(RET_WIDTH // LANES):
            z = zg[:, j * LANES:(j + 1) * LANES]
            if grp in (0, 1):
                z = _rotate(z, cr_ref[...], slr_ref[...], shr_ref[...], DK_RET // 2)
                if grp == 1:
                    z = z * (DK_RET ** -0.5)
            elif grp in (4, 5):
                z = _rotate(z, ca_ref[...], sla_ref[...], sha_ref[...], ROT_DIMS // 2)
            o_ref[:, j * LANES:(j + 1) * LANES] = z.astype(o_ref.dtype)
            if t_refs and grp in (5, 6):
                zt = z.T
                for hh in range(HEADS_PER_TILE):
                    t_refs[grp - 5][0, HEADS_PER_TILE * j + hh] = zt[hh * HD_ATT:(hh + 1) * HD_ATT]


def _rot_tables(pos, n_rot, base):
    half = n_rot // 2
    inv = base ** (-jnp.arange(half, dtype=F32) / half)
    ang = pos.astype(F32)[:, None] * inv[None, :]
    cos, sin = jnp.cos(ang), jnp.sin(ang)
    l = np.arange(LANES) % DK_RET
    idx = np.where(l < half, l, np.where(l < n_rot, l - half, 0))
    lo = jnp.asarray(l < half)
    hi = jnp.asarray((l >= half) & (l < n_rot))
    cos_f = jnp.where(jnp.asarray(l < n_rot), cos[:, idx], 1.0)
    s_lo = jnp.where(lo, -sin[:, idx], 0.0)
    s_hi = jnp.where(hi, sin[:, idx], 0.0)
    return cos_f, s_lo, s_hi


def _proj_call(x2d, ln_g, ln_b, w_in_b, tabs, tm, tab_blocks, out_dtypes, seq_for_transposed=None):
    n = x2d.shape[0]
    row = lambda i: (i, 0)
    tab = lambda i: (i % tab_blocks, 0)
    tab_spec = pl.BlockSpec((tm, LANES), tab)
    out_specs = [pl.BlockSpec((tm, RET_WIDTH), row)] * 7
    out_shape = [jax.ShapeDtypeStruct((n, RET_WIDTH), dt) for dt in out_dtypes]
    if seq_for_transposed is not None:
        tps = seq_for_transposed // tm
        out_specs += [pl.BlockSpec((1, H_ATT, HD_ATT, tm), lambda i: (i // tps, 0, 0, i % tps))] * 2
        out_shape += [jax.ShapeDtypeStruct((n // seq_for_transposed, H_ATT, HD_ATT, seq_for_transposed), F32)] * 2
    return pl.pallas_call(
        _proj_kernel,
        grid=(n // tm,),
        in_specs=[pl.BlockSpec((tm, D_MODEL), row), _const_spec((1, D_MODEL)), _const_spec((1, D_MODEL)),
                  _const_spec((D_MODEL, PROJ_WIDTH))] + [tab_spec] * 6,
        out_specs=out_specs,
        out_shape=out_shape,
        compiler_params=pltpu.CompilerParams(dimension_semantics=("parallel",), vmem_limit_bytes=VMEM_LIMIT),
    )(x2d, ln_g, ln_b, w_in_b, *tabs)


COMBINE_ROWS = 256


def _round_robin(entries):
    live = list(entries)
    while live:
        for entry in list(live):
            for _ in range(entry[1]):
                if next(entry[0], live) is live:
                    live.remove(entry)
                    break
                yield


def _skewed(items, n_stages):
    for t in range(len(items) + n_stages - 1):
        for stage in range(n_stages):
            if 0 <= t - stage < len(items):
                next(items[t - stage], None)
        yield


def _ret_body(q_ref, k_ref, v_ref, g_ref, r_ref, st_ref, hp, n_chunks):
    C = RET_CHUNK
    lane1 = lax.broadcasted_iota(jnp.int32, (1, LANES), 1)
    lg_lane = _select_log_gamma(HEADS_PER_TILE * hp + (lane1 >= DK_RET).astype(jnp.int32))

    ri = lax.broadcasted_iota(jnp.int32, (2 * C, C), 0)
    ci = lax.broadcasted_iota(jnp.int32, (2 * C, C), 1)
    rel = (jnp.where(ri >= C, ri - C, ri) - ci).astype(F32)
    lg_rows = _select_log_gamma(HEADS_PER_TILE * hp + (ri >= C).astype(jnp.int32))
    decay = jnp.where(rel >= 0, jnp.exp(lg_rows * jnp.maximum(rel, 0.0)), 0.0)

    tok = lax.broadcasted_iota(jnp.int32, (C, LANES), 0).astype(F32)
    cross_dec = jnp.exp(lg_lane * (tok + 1.0))
    k_dec = jnp.exp(lg_lane * (C - 1.0 - tok))
    sr = lax.broadcasted_iota(jnp.int32, (LANES, LANES), 0)
    sc = lax.broadcasted_iota(jnp.int32, (LANES, LANES), 1)
    same_head = ((sr >= DK_RET) == (sc >= DK_RET)).astype(F32)
    state_dec = jnp.exp(_select_log_gamma(HEADS_PER_TILE * hp + (sr >= DK_RET).astype(jnp.int32)) * float(C))

    states = [jnp.zeros((LANES, LANES), F32)]

    def chunk(c):
        first = lax.broadcasted_iota(jnp.int32, (1, LANES), 1) < DK_RET
        off = c * C
        q = q_ref[pl.ds(off, C), :]
        k = k_ref[pl.ds(off, C), :]
        v = v_ref[pl.ds(off, C), :]
        zero = jnp.zeros_like(q)
        qs = jnp.concatenate([jnp.where(first, q, zero), jnp.where(first, zero, q)], axis=0)
        s = lax.dot_general(qs, k.astype(BF16), (((1,), (1,)), ((), ())), preferred_element_type=F32)
        upd = lax.dot_general((k * k_dec).astype(BF16), v, (((0,), (0,)), ((), ())),
                              preferred_element_type=F32)
        states.append(state_dec * states[c] + same_head * upd)
        yield
        pv = jnp.dot((s * decay).astype(BF16), v, preferred_element_type=F32)
        cross = jnp.dot(q, states[c].astype(BF16), preferred_element_type=F32) * cross_dec
        yield
        o = jnp.where(first, pv[:C], pv[C:]) + cross
        yield
        s_a = jnp.sum(jnp.where(first, o, 0.0), axis=-1, keepdims=True)
        s_b = jnp.sum(jnp.where(first, 0.0, o), axis=-1, keepdims=True)
        xc = o - jnp.where(first, s_a, s_b) * (1.0 / DK_RET)
        sq = xc * xc
        yield
        v_a = jnp.sum(jnp.where(first, sq, 0.0), axis=-1, keepdims=True)
        v_b = jnp.sum(jnp.where(first, 0.0, sq), axis=-1, keepdims=True)
        rn = xc * lax.rsqrt(jnp.where(first, v_a, v_b) * (1.0 / DK_RET) + LN_EPS)
        g = g_ref[pl.ds(off, C), :]
        r_ref[pl.ds(off, C), :] = (rn * (g * _sigmoid(g))).astype(r_ref.dtype)

    yield from _skewed([chunk(c) for c in range(n_chunks)], 5)
    st_ref[0, 0] = states[n_chunks][:DK_RET, :DK_RET]
    st_ref[0, 1] = states[n_chunks][DK_RET:, DK_RET:]


def _needs_pitch(dil):
    return dil % 8 == 0


def _att_body(q_ref, k_ref, v_ref, o_ref, num_ref, den_ref, max_ref, qp_ref, kp_ref, vp_ref, seq):
    QB = Q_BLOCK
    pitched = [dil for _, dil in DILATIONS if _needs_pitch(dil)]
    assert len(pitched) <= 1
    for dil in pitched:
        for g in range(seq // dil):
            for src, dst in ((q_ref, qp_ref), (k_ref, kp_ref), (v_ref, vp_ref)):
                dst[g * (dil + 1):g * (dil + 1) + dil, :] = src[g * dil:(g + 1) * dil, :]
    ri = lax.broadcasted_iota(jnp.int32, (2 * QB, 2 * QB), 0)
    ci = lax.broadcasted_iota(jnp.int32, (2 * QB, 2 * QB), 1)
    qi = jnp.where(ri >= QB, ri - QB, ri)
    valid_band = ((ci < QB) & (ci >= qi)) | ((ci >= QB) & (ci - QB <= qi))
    bias_band = jnp.where(valid_band, 0.0, NEG_BIG)
    rd = lax.broadcasted_iota(jnp.int32, (2 * QB, QB), 0)
    cd = lax.broadcasted_iota(jnp.int32, (2 * QB, QB), 1)
    bias_diag = jnp.where(cd <= jnp.where(rd >= QB, rd - QB, rd), 0.0, NEG_BIG)
    ones = jnp.ones((2 * QB, LANES), BF16)

    def unit(br, dil, q_start, k_start, n_keys, bias):
        lane = lax.broadcasted_iota(jnp.int32, (1, LANES), 1)
        first = lane < HD_ATT
        rows_q = pl.ds(q_start, QB, stride=dil) if dil > 1 else pl.ds(q_start, QB)
        if _needs_pitch(dil):
            src_q, src_k, src_v = qp_ref, kp_ref, vp_ref
            rows_qs = pl.ds(q_start, QB, stride=dil + 1)
            rows_k = pl.ds(k_start, n_keys, stride=dil + 1)
        else:
            src_q, src_k, src_v = q_ref, k_ref, v_ref
            rows_qs = rows_q
            rows_k = pl.ds(k_start, n_keys, stride=dil) if dil > 1 else pl.ds(k_start, n_keys)
        q = src_q[rows_qs, :] * ATT_SCALE
        qs = jnp.concatenate([jnp.where(first, q, 0.0), jnp.where(first, 0.0, q)], axis=0).astype(BF16)
        kk = src_k[rows_k, :].astype(BF16)
        s = lax.dot_general(qs, kk, (((1,), (1,)), ((), ())), preferred_element_type=F32)
        yield
        s = s + bias
        m = jnp.max(s, axis=-1, keepdims=True)
        yield
        p = jnp.exp(s - m).astype(BF16)
        yield
        vv = src_v[rows_k, :].astype(BF16)
        pv = jnp.dot(p, jnp.concatenate([vv, ones[:n_keys]], axis=1), preferred_element_type=F32)
        yield
        mb = jnp.broadcast_to(m, (2 * QB, LANES))
        num_ref[br, rows_q, :] = jnp.where(first, pv[:QB, :LANES], pv[QB:, :LANES])
        den_ref[br, rows_q, :] = jnp.where(first, pv[:QB, LANES:], pv[QB:, LANES:])
        max_ref[br, rows_q, :] = jnp.where(first, mb[:QB], mb[QB:])

    units = []
    for br, (window, dil) in enumerate(DILATIONS):
        assert window // dil == QB
        sub_len = seq // dil
        n_blocks = sub_len // QB
        assert not (_needs_pitch(dil) and n_blocks > 1)
        for r in range(dil):
            units.append(unit(br, dil, r, r, QB, bias_diag))
            for n in range(1, n_blocks):
                k_start = r + dil * (n - 1) * QB
                units.append(unit(br, dil, k_start + dil * QB, k_start, 2 * QB, bias_band))
    yield from _skewed(units, 5)

    for i in range(seq // COMBINE_ROWS):
        rows = pl.ds(i * COMBINE_ROWS, COMBINE_ROWS)
        m0, m1, m2 = max_ref[0, rows, :], max_ref[1, rows, :], max_ref[2, rows, :]
        mx = jnp.maximum(jnp.maximum(m0, m1), m2)
        e0, e1, e2 = jnp.exp(m0 - mx), jnp.exp(m1 - mx), jnp.exp(m2 - mx)
        num = e0 * num_ref[0, rows, :] + e1 * num_ref[1, rows, :] + e2 * num_ref[2, rows, :]
        den = e0 * den_ref[0, rows, :] + e1 * den_ref[1, rows, :] + e2 * den_ref[2, rows, :]
        o_ref[rows, :] = (num / den).astype(o_ref.dtype)
        yield


ATT_STEPS_PER_RET_CHUNK = 1


def _mixer_kernel(qr_ref, kr_ref, vr_ref, gr_ref, qa_ref, ka_ref, va_ref, r_ref, st_ref, o_ref,
                  num_ref, den_ref, max_ref, qp_ref, kp_ref, vp_ref, *, seq):
    for _ in _round_robin([
            (_ret_body(qr_ref, kr_ref, vr_ref, gr_ref, r_ref, st_ref, pl.program_id(1), seq // RET_CHUNK), 1),
            (_att_body(qa_ref, ka_ref, va_ref, o_ref, num_ref, den_ref, max_ref, qp_ref, kp_ref, vp_ref, seq),
             ATT_STEPS_PER_RET_CHUNK)]):
        pass


def _mixer_call(qr, kr, vr, gr, qa, ka, va, batch, seq):
    assert H_RET == H_ATT and seq % RET_CHUNK == 0 and seq % COMBINE_ROWS == 0
    blk = pl.BlockSpec((seq, LANES), lambda b, p: (b, p))
    n_br = len(DILATIONS)
    pitched_rows = max([seq // dil * (dil + 1) for _, dil in DILATIONS if _needs_pitch(dil)] + [8])
    return pl.pallas_call(
        functools.partial(_mixer_kernel, seq=seq),
        grid=(batch, H_RET // HEADS_PER_TILE),
        in_specs=[blk] * 7,
        out_specs=[blk, pl.BlockSpec((1, HEADS_PER_TILE, DK_RET, DK_RET), lambda b, p: (b, p, 0, 0)), blk],
        out_shape=[jax.ShapeDtypeStruct((batch * seq, RET_WIDTH), BF16),
                   jax.ShapeDtypeStruct((batch, H_RET, DK_RET, DK_RET), F32),
                   jax.ShapeDtypeStruct((batch * seq, ATT_WIDTH), BF16)],
        scratch_shapes=[pltpu.VMEM((n_br, seq, LANES), F32)] * 3 + [pltpu.VMEM((pitched_rows, LANES), F32)] * 3,
        compiler_params=pltpu.CompilerParams(dimension_semantics=("parallel", "parallel"),
                                             vmem_limit_bytes=VMEM_LIMIT),
    )(qr, kr, vr, gr, qa, ka, va)


def _post_kernel(*refs, tm, tiles_per_seq, shift):
    (x_ref, r_ref, a_ref, p_ref, lng_ref, lnb_ref, wo_ref, g1_ref, b1_ref, wup_ref, cw_ref, cb_ref,
     wdn_ref, wpg_ref, wpp_ref, g2_ref, b2_ref) = refs[:17]
    if shift:
        y_ref, cs_ref, h1b_s, halo_s = refs[17:]
    else:
        pre0_ref, pre1_ref, y_ref, cs_ref, h1b_s = refs[17:]

    h = _layer_norm(x_ref[...], lng_ref[...], lnb_ref[...])
    mix = (jnp.dot(r_ref[...].astype(BF16), wo_ref[:RET_WIDTH, :], preferred_element_type=F32)
           + jnp.dot(a_ref[...].astype(BF16), wo_ref[RET_WIDTH:, :], preferred_element_type=F32))
    h1 = _layer_norm(ALPHA * h + mix, g1_ref[...], b1_ref[...])
    h1b_s[...] = h1.astype(BF16)
    gate = _sigmoid(jnp.dot(h1b_s[...], wpg_ref[...], preferred_element_type=F32))
    y_ref[...] = ALPHA * h1 + gate * jnp.dot(p_ref[...].astype(BF16), wpp_ref[...], preferred_element_type=F32)

    if shift:
        @pl.when(pl.program_id(0) % tiles_per_seq == 0)
        def _():
            halo_s[...] = jnp.zeros_like(halo_s)
        row8 = lax.broadcasted_iota(jnp.int32, (8, FF_CHUNK), 0)

    n_chunks = D_FF // FF_CHUNK
    partials, acts = [], []
    cols_of = lambda part, j: slice(part * D_FF + j * FF_CHUNK, part * D_FF + (j + 1) * FF_CHUNK)
    kw = D_MODEL // UP_K_SPLIT
    grp = min(tm, ACT_GROUP_ROWS)
    u_of = {}

    def up_gen(j):
        for part in (0, 1):
            acc = None
            for kq in range(UP_K_SPLIT):
                d = jnp.dot(h1b_s[:, kq * kw:(kq + 1) * kw], wup_ref[kq * kw:(kq + 1) * kw, cols_of(part, j)],
                            preferred_element_type=F32)
                acc = d if acc is None else acc + d
                yield
            u_of[j, part] = acc

    def act_gen(j, out):
        pieces = []
        for r0 in range(0, tm, grp):
            conv = []
            for part in (0, 1):
                cols = cols_of(part, j)
                u = u_of[j, part][r0:r0 + grp]
                if shift:
                    above = halo_s[:, cols] if r0 == 0 else u_of[j, part][r0 - 8:r0]

                    def shifted(k, u=u, above=above):
                        rolled = pltpu.roll(u, k, 0)
                        top = jnp.where(row8 < k, pltpu.roll(above, k, 0), rolled[0:8])
                        return jnp.concatenate([top, rolled[8:]], axis=0)

                    prev2, prev1 = shifted(2), shifted(1)
                    if r0 + grp == tm:
                        halo_s[:, cols] = u[grp - 8:, :]
                        cs_ref[0, :, cols] = u[grp - (CONV_W - 1):, :]
                else:
                    cs_ref[r0:r0 + grp, cols] = u
                    prev2 = pre0_ref[r0:r0 + grp, cols]
                    prev1 = pre1_ref[r0:r0 + grp, cols]
                conv.append(cb_ref[:, cols] + cw_ref[0:1, cols] * prev2 + cw_ref[1:2, cols] * prev1
                            + cw_ref[2:3, cols] * u)
                yield
            pieces.append((conv[0] * _gelu_tanh(conv[1])).astype(BF16))
            yield
        del u_of[j, 0], u_of[j, 1]
        out.append(jnp.concatenate(pieces, axis=0))

    for j in range(n_chunks + 1):
        for _ in _round_robin(([(up_gen(j), 1)] if j < n_chunks else [])
                              + ([(act_gen(j - 1, acts), ACT_STEPS_PER_UP_STEP)] if j else [])):
            pass
        if j and (len(acts) == DOWN_GROUP or j == n_chunks):
            r0 = (j - len(acts)) * FF_CHUNK
            partials.append(jnp.dot(jnp.concatenate(acts, axis=1), wdn_ref[r0:j * FF_CHUNK, :],
                                    preferred_element_type=F32))
            acts = []

    y_ref[...] = _layer_norm(y_ref[...] + sum(partials[1:], partials[0]), g2_ref[...], b2_ref[...])


def _post_call(x2d, r2d, a2d, p2d, weights, tm, tiles_per_seq, prefix=None):
    n = x2d.shape[0]
    shift = prefix is None
    two_f = 2 * D_FF
    row = lambda i: (i, 0)
    in_specs = [pl.BlockSpec((tm, D_MODEL), row), pl.BlockSpec((tm, RET_WIDTH), row),
                pl.BlockSpec((tm, ATT_WIDTH), row), pl.BlockSpec((tm, PLE_DIM), row)]
    in_specs += [_const_spec(w.shape) for w in weights]
    args = [x2d, r2d, a2d, p2d, *weights]
    scratch = [pltpu.VMEM((tm, D_MODEL), BF16)]
    if shift:
        out_specs = [pl.BlockSpec((tm, D_MODEL), row),
                     pl.BlockSpec((1, CONV_W - 1, two_f), lambda i: (i // tiles_per_seq, 0, 0))]
        out_shape = [jax.ShapeDtypeStruct((n, D_MODEL), F32),
                     jax.ShapeDtypeStruct((n // (tm * tiles_per_seq), CONV_W - 1, two_f), F32)]
        scratch += [pltpu.VMEM((8, two_f), F32)]
    else:
        in_specs += [pl.BlockSpec((tm, two_f), row)] * 2
        args += list(prefix)
        out_specs = [pl.BlockSpec((tm, D_MODEL), row), pl.BlockSpec((tm, two_f), row)]
        out_shape = [jax.ShapeDtypeStruct((n, D_MODEL), F32), jax.ShapeDtypeStruct((n, two_f), F32)]
    return pl.pallas_call(
        functools.partial(_post_kernel, tm=tm, tiles_per_seq=tiles_per_seq, shift=shift),
        grid=(n // tm,),
        in_specs=in_specs,
        out_specs=out_specs,
        out_shape=out_shape,
        scratch_shapes=scratch,
        compiler_params=pltpu.CompilerParams(dimension_semantics=("arbitrary",), vmem_limit_bytes=VMEM_LIMIT),
    )(*args)


def _sample_mix_kernel(qr_ref, kr_ref, vr_ref, gr_ref, st_ref, qa_ref, ka_ref, qat_ref, vat_ref,
                       kt_ref, vt_ref, r_ref, att_ref, nst_ref, s_s, pc_s):
    hrow = lax.broadcasted_iota(jnp.int32, (H_RET, RET_WIDTH), 0)
    hlane = lax.broadcasted_iota(jnp.int32, (H_RET, RET_WIDTH), 1) // DK_RET
    own = hrow == hlane

    qm = jnp.where(own, qr_ref[0], 0.0)
    km = jnp.where(own, kr_ref[0], 0.0)
    v8 = vr_ref[0]
    g8 = gr_ref[0]
    st = st_ref[0]
    lg8 = _select_log_gamma(lax.broadcasted_iota(jnp.int32, (H_RET, 1), 0))
    cross = jnp.dot(qm.astype(BF16), st.astype(BF16), preferred_element_type=F32) * jnp.exp(lg8)
    qk = jnp.sum(qm * km, axis=-1, keepdims=True)
    o = qk * v8 + cross
    lg_rows = _select_log_gamma(lax.broadcasted_iota(jnp.int32, (H_RET * DK_RET, 1), 0) // DK_RET)
    outer = lax.dot_general(km, v8, (((0,), (0,)), ((), ())), preferred_element_type=F32,
                            precision=lax.Precision.HIGHEST)
    nst_ref[0] = jnp.exp(lg_rows) * st + outer
    mu = jnp.mean(o, axis=-1, keepdims=True)
    xc = o - mu
    var = jnp.mean(xc * xc, axis=-1, keepdims=True)
    r_ref[0] = xc * lax.rsqrt(var + LN_EPS) * (g8 * _sigmoid(g8))

    n_past = kt_ref.shape[-1]
    s_new = jnp.sum(qa_ref[0] * ka_ref[0], axis=-1, keepdims=True) * ATT_SCALE
    qt = qat_ref[0] * ATT_SCALE
    for h in range(H_ATT):
        s_s[h:h + 1, :] = jnp.sum(kt_ref[h] * qt[:, h:h + 1], axis=0, keepdims=True)
    tok = lax.broadcasted_iota(jnp.int32, (H_ATT, n_past), 1)
    stats = []
    for window, dil in DILATIONS:
        lo = n_past - window
        sb = s_s[:, lo:] + jnp.where(tok[:, lo:] % dil == n_past % dil, 0.0, NEG_BIG)
        m = jnp.maximum(jnp.max(sb, axis=-1, keepdims=True), s_new)
        p = jnp.exp(sb - m)
        p_new = jnp.exp(s_new - m)
        den = jnp.sum(p, axis=-1, keepdims=True) + p_new
        stats.append((lo, p, p_new, den, m + jnp.log(den)))
    mx = jnp.maximum(jnp.maximum(stats[0][4], stats[1][4]), stats[2][4])
    es = [jnp.exp(st_[4] - mx) for st_ in stats]
    tot = es[0] + es[1] + es[2]
    coef = [e / (tot * st_[3]) for e, st_ in zip(es, stats)]
    w_new = coef[0] * stats[0][2] + coef[1] * stats[1][2] + coef[2] * stats[2][2]
    order = sorted(range(len(stats)), key=lambda i: stats[i][0])
    assert stats[order[0]][0] == 0
    pc_s[...] = coef[order[0]] * stats[order[0]][1]
    for i in order[1:]:
        lo = stats[i][0]
        pc_s[:, lo:] = pc_s[:, lo:] + coef[i] * stats[i][1]
    vt_new = vat_ref[0]
    for h in range(H_ATT):
        col = jnp.sum(vt_ref[h] * pc_s[h:h + 1, :], axis=-1, keepdims=True)
        att_ref[0, :, h:h + 1] = col + w_new[h:h + 1, :] * vt_new[:, h:h + 1]


def _sample_mix_call(qr, kr, vr, gr, state, qa, ka, va, cache_k, cache_v):
    nb = qr.shape[0]
    n_past = cache_k.shape[-1]
    assert all(w <= n_past and n_past % d == 0 for w, d in DILATIONS) and max(w for w, _ in DILATIONS) == n_past
    row3 = pl.BlockSpec((1, 1, RET_WIDTH), lambda b: (b, 0, 0))
    head3 = pl.BlockSpec((1, H_RET, DK_RET), lambda b: (b, 0, 0))
    col3 = pl.BlockSpec((1, HD_ATT, H_ATT), lambda b: (b, 0, 0))
    st_spec = pl.BlockSpec((1, H_RET * DK_RET, DK_RET), lambda b: (b, 0, 0))
    cache_spec = pl.BlockSpec((None, H_ATT, HD_ATT, n_past), lambda b: (b, 0, 0, 0))
    r3 = lambda a: a.reshape(nb, 1, RET_WIDTH)
    h3 = lambda a: a.reshape(nb, H_RET, DK_RET)
    t3 = lambda a: jnp.swapaxes(h3(a), 1, 2)
    r_s, att_t, nst = pl.pallas_call(
        _sample_mix_kernel,
        grid=(nb,),
        in_specs=[row3, row3, head3, head3, st_spec, head3, head3, col3, col3, cache_spec, cache_spec],
        out_specs=[head3, col3, st_spec],
        out_shape=[jax.ShapeDtypeStruct((nb, H_RET, DK_RET), F32),
                   jax.ShapeDtypeStruct((nb, HD_ATT, H_ATT), F32),
                   jax.ShapeDtypeStruct((nb, H_RET * DK_RET, DK_RET), F32)],
        scratch_shapes=[pltpu.VMEM((H_ATT, n_past), F32), pltpu.VMEM((H_ATT, n_past), F32)],
        compiler_params=pltpu.CompilerParams(dimension_semantics=("parallel",), vmem_limit_bytes=VMEM_LIMIT),
    )(r3(qr), r3(kr), h3(vr), h3(gr), state.reshape(nb, H_RET * DK_RET, DK_RET), h3(qa), h3(ka), t3(qa), t3(va),
      cache_k, cache_v)
    return r_s, jnp.swapaxes(att_t, 1, 2), nst


def kernel(x_prompt, x_sample, cache_k_win, cache_v_win, state_ret, state_conv, p_prompt, p_sample,
           ln_in_g, ln_in_b, w_in, w_out, ln1_g, ln1_b, w_up, conv_w, conv_b, w_down,
           w_ple_gate, w_ple_proj, ln2_g, ln2_b):
    B, S, _ = x_prompt.shape
    NB, T, _ = x_sample.shape
    assert T == 1 and w_in.shape[0] == DEPTH == 1 and S % PROJ_TILE == 0 and S % POST_TILE == 0
    two_f = 2 * D_FF
    vec = lambda a: a.reshape(1, -1)
    w_in_b = w_in[0].astype(BF16)
    post_w = (vec(ln_in_g), vec(ln_in_b), w_out[0].astype(BF16), vec(ln1_g[0]), vec(ln1_b[0]),
              w_up[0].astype(BF16), conv_w[0], vec(conv_b[0]), w_down[0].astype(BF16),
              w_ple_gate[0].astype(BF16), w_ple_proj[0].astype(BF16), vec(ln2_g[0]), vec(ln2_b[0]))

    pos_p = jnp.arange(S, dtype=jnp.int32)
    tabs_p = _rot_tables(pos_p, DK_RET, RET_ROPE_BASE) + _rot_tables(pos_p, ROT_DIMS, ATT_ROPE_THETA)
    xp = x_prompt.reshape(B * S, D_MODEL)
    qr, kr, vr, gr, qa, ka, va, ka_t, va_t = _proj_call(
        xp, vec(ln_in_g), vec(ln_in_b), w_in_b, tabs_p, PROJ_TILE, S // PROJ_TILE,
        (BF16, F32, BF16, F32, F32, F32, F32), seq_for_transposed=S)
    r_p, ret_fin, att_p = _mixer_call(qr, kr, vr, gr, qa, ka, va, B, S)
    y_p, conv_p = _post_call(xp, r_p, att_p, p_prompt[0].reshape(B * S, PLE_DIM), post_w,
                             POST_TILE, S // POST_TILE)
    keep = min(WINDOW_MAX, S)
    k_win_p = jnp.transpose(ka_t, (0, 3, 1, 2))[:, S - keep:]
    v_win_p = jnp.transpose(va_t, (0, 3, 1, 2))[:, S - keep:]

    pos_s = jnp.full((NB,), PAST_LEN, jnp.int32)
    tabs_s = _rot_tables(pos_s, DK_RET, RET_ROPE_BASE) + _rot_tables(pos_s, ROT_DIMS, ATT_ROPE_THETA)
    xs = x_sample.reshape(NB, D_MODEL)
    sqr, skr, svr, sgr, sqa, ska, sva = _proj_call(xs, vec(ln_in_g), vec(ln_in_b), w_in_b, tabs_s, NB, 1,
                                                   (F32,) * 7)
    r_s, att_s, nst = _sample_mix_call(sqr, skr, svr, sgr, state_ret[0], sqa, ska, sva,
                                       jnp.transpose(cache_k_win[0], (0, 2, 3, 1)),
                                       jnp.transpose(cache_v_win[0], (0, 2, 3, 1)))
    y_s, u_s = _post_call(xs, r_s.reshape(NB, RET_WIDTH), att_s.reshape(NB, ATT_WIDTH), p_sample[0].reshape(NB, PLE_DIM),
                          post_w, NB, 1, prefix=(state_conv[0][:, 0], state_conv[0][:, 1]))
    conv_s = jnp.stack([state_conv[0][:, 1], u_s], axis=1)

    return (y_p.reshape(B, S, D_MODEL), y_s.reshape(NB, 1, D_MODEL),
            k_win_p[None], v_win_p[None], ret_fin[None], conv_p[None],
            ska.reshape(1, NB, 1, H_ATT, HD_ATT), sva.reshape(1, NB, 1, H_ATT, HD_ATT),
            nst.reshape(1, NB, H_RET, DK_RET, DK_RET), conv_s[None])
```

```python
import functools
import math

import numpy as np
import jax
import jax.numpy as jnp
from jax import lax
from jax.experimental import pallas as pl
from jax.experimental.pallas import tpu as pltpu

F32 = jnp.float32
BF16 = jnp.bfloat16

D_MODEL = 1024
PAST_LEN = 16384
H_RET = 8
DK_RET = 64
RET_WIDTH = 512
RET_CHUNK = 128
RET_ROPE_BASE = 10000.0
H_ATT = 8
HD_ATT = 64
ATT_WIDTH = 512
ATT_ROPE_THETA = 500000.0
ROT_DIMS = HD_ATT // 4
DILATIONS = ((128, 1), (512, 4), (2048, 16))
WINDOW_MAX = 2048
Q_BLOCK = 128
PROJ_WIDTH = 4 * RET_WIDTH + 3 * ATT_WIDTH
D_FF = 2816
CONV_W = 3
PLE_DIM = 256
LN_EPS = 1e-5
DEPTH = 1
ALPHA = (2 * DEPTH) ** 0.25

LANES = 128
HEADS_PER_TILE = LANES // DK_RET
NEG_BIG = -1e30
LOG_GAMMA = tuple(math.log(1.0 - 2.0 ** (-5.0 - h)) for h in range(H_RET))
ATT_SCALE = HD_ATT ** -0.5
VMEM_LIMIT = 56 * 1024 * 1024

PROJ_TILE = 512
POST_TILE = 512
FF_CHUNK = 256
DOWN_GROUP = 3
UP_K_SPLIT = 2
ACT_GROUP_ROWS = 128
ACT_STEPS_PER_UP_STEP = 3


def _layer_norm(x, g, b):
    mu = jnp.mean(x, axis=-1, keepdims=True)
    xc = x - mu
    var = jnp.mean(xc * xc, axis=-1, keepdims=True)
    return xc * lax.rsqrt(var + LN_EPS) * g + b


def _sigmoid(x):
    return 1.0 / (1.0 + jnp.exp(-x))


def _gelu_tanh(x):
    return 0.5 * x * (1.0 + jnp.tanh(math.sqrt(2.0 / math.pi) * (x + 0.044715 * (x * x * x))))


def _select_log_gamma(head_idx):
    out = jnp.zeros(head_idx.shape, F32)
    for h in range(H_RET):
        out = jnp.where(head_idx == h, LOG_GAMMA[h], out)
    return out


def _const_spec(shape):
    nd = len(shape)
    return pl.BlockSpec(shape, lambda *_: (0,) * nd, pipeline_mode=pl.Buffered(1))


def _rotate(z, cos, s_lo, s_hi, half):
    up = pltpu.roll(z, LANES - half, 1)
    dn = pltpu.roll(z, half, 1)
    return z * cos + up * s_lo + dn * s_hi


PROJ_GROUP_ORDER = (5, 6, 0, 1, 4, 3, 2)


def _proj_kernel(x_ref, g_ref, b_ref, w_ref, cr_ref, slr_ref, shr_ref, ca_ref, sla_ref, sha_ref,
                 qr_ref, kr_ref, vr_ref, gr_ref, qa_ref, ka_ref, va_ref, *t_refs):
    outs = (qr_ref, kr_ref, vr_ref, gr_ref, qa_ref, ka_ref, va_ref)
    tm = x_ref.shape[0]
    cols = lambda grp: slice(grp * RET_WIDTH, (grp + 1) * RET_WIDTH)
    part = tm // 2 if tm % 32 == 0 else tm
    hb_parts = [_layer_norm(x_ref[r0:r0 + part], g_ref[...], b_ref[...]).astype(BF16) for r0 in range(0, tm, part)]
    z_next = jnp.concatenate([jnp.dot(hp, w_ref[:, cols(PROJ_GROUP_ORDER[0])], preferred_element_type=F32)
                              for hp in hb_parts], axis=0)
    hb = jnp.concatenate(hb_parts, axis=0)
    for i, grp in enumerate(PROJ_GROUP_ORDER):
        o_ref = outs[grp]
        zg = z_next
        if i + 1 < len(PROJ_GROUP_ORDER):
            z_next = jnp.dot(hb, w_ref[:, cols(PROJ_GROUP_ORDER[i + 1])], preferred_element_type=F32)
        for j in range(RET_WIDTH // LANES):
            z = zg[:, j * LANES:(j + 1) * LANES]
            if grp in (0, 1):
                z = _rotate(z, cr_ref[...], slr_ref[...], shr_ref[...], DK_RET // 2)
                if grp == 1:
                    z = z * (DK_RET ** -0.5)
            elif grp in (4, 5):
                z = _rotate(z, ca_ref[...], sla_ref[...], sha_ref[...], ROT_DIMS // 2)
            o_ref[:, j * LANES:(j + 1) * LANES] = z.astype(o_ref.dtype)
            if t_refs and grp in (5, 6):
                zt = z.T
                for hh in range(HEADS_PER_TILE):
                    t_refs[grp - 5][0, HEADS_PER_TILE * j + hh] = zt[hh * HD_ATT:(hh + 1) * HD_ATT]


def _rot_tables(pos, n_rot, base):
    half = n_rot // 2
    inv = base ** (-jnp.arange(half, dtype=F32) / half)
    ang = pos.astype(F32)[:, None] * inv[None, :]
    cos, sin = jnp.cos(ang), jnp.sin(ang)
    l = np.arange(LANES) % DK_RET
    idx = np.where(l < half, l, np.where(l < n_rot, l - half, 0))
    lo = jnp.asarray(l < half)
    hi = jnp.asarray((l >= half) & (l < n_rot))
    cos_f = jnp.where(jnp.asarray(l < n_rot), cos[:, idx], 1.0)
    s_lo = jnp.where(lo, -sin[:, idx], 0.0)
    s_hi = jnp.where(hi, sin[:, idx], 0.0)
    return cos_f, s_lo, s_hi


def _proj_call(x2d, ln_g, ln_b, w_in_b, tabs, tm, tab_blocks, out_dtypes, seq_for_transposed=None):
    n = x2d.shape[0]
    row = lambda i: (i, 0)
    tab = lambda i: (i % tab_blocks, 0)
    tab_spec = pl.BlockSpec((tm, LANES), tab)
    out_specs = [pl.BlockSpec((tm, RET_WIDTH), row)] * 7
    out_shape = [jax.ShapeDtypeStruct((n, RET_WIDTH), dt) for dt in out_dtypes]
    if seq_for_transposed is not None:
        tps = seq_for_transposed // tm
        out_specs += [pl.BlockSpec((1, H_ATT, HD_ATT, tm), lambda i: (i // tps, 0, 0, i % tps))] * 2
        out_shape += [jax.ShapeDtypeStruct((n // seq_for_transposed, H_ATT, HD_ATT, seq_for_transposed), F32)] * 2
    return pl.pallas_call(
        _proj_kernel,
        grid=(n // tm,),
        in_specs=[pl.BlockSpec((tm, D_MODEL), row), _const_spec((1, D_MODEL)), _const_spec((1, D_MODEL)),
                  _const_spec((D_MODEL, PROJ_WIDTH))] + [tab_spec] * 6,
        out_specs=out_specs,
        out_shape=out_shape,
        compiler_params=pltpu.CompilerParams(dimension_semantics=("parallel",), vmem_limit_bytes=VMEM_LIMIT),
    )(x2d, ln_g, ln_b, w_in_b, *tabs)


COMBINE_ROWS = 256


def _round_robin(entries):
    live = list(entries)
    while live:
        for entry in list(live):
            for _ in range(entry[1]):
                if next(entry[0], live) is live:
                    live.remove(entry)
                    break
                yield


def _skewed(items, n_stages):
    for t in range(len(items) + n_stages - 1):
        for stage in range(n_stages):
            if 0 <= t - stage < len(items):
                next(items[t - stage], None)
        yield


def _ret_body(q_ref, k_ref, v_ref, g_ref, r_ref, st_ref, hp, n_chunks):
    C = RET_CHUNK
    lane1 = lax.broadcasted_iota(jnp.int32, (1, LANES), 1)
    lg_lane = _select_log_gamma(HEADS_PER_TILE * hp + (lane1 >= DK_RET).astype(jnp.int32))

    ri = lax.broadcasted_iota(jnp.int32, (2 * C, C), 0)
    ci = lax.broadcasted_iota(jnp.int32, (2 * C, C), 1)
    rel = (jnp.where(ri >= C, ri - C, ri) - ci).astype(F32)
    lg_rows = _select_log_gamma(HEADS_PER_TILE * hp + (ri >= C).astype(jnp.int32))
    decay = jnp.where(rel >= 0, jnp.exp(lg_rows * jnp.maximum(rel, 0.0)), 0.0)

    tok = lax.broadcasted_iota(jnp.int32, (C, LANES), 0).astype(F32)
    cross_dec = jnp.exp(lg_lane * (tok + 1.0))
    k_dec = jnp.exp(lg_lane * (C - 1.0 - tok))
    sr = lax.broadcasted_iota(jnp.int32, (LANES, LANES), 0)
    sc = lax.broadcasted_iota(jnp.int32, (LANES, LANES), 1)
    same_head = ((sr >= DK_RET) == (sc >= DK_RET)).astype(F32)
    state_dec = jnp.exp(_select_log_gamma(HEADS_PER_TILE * hp + (sr >= DK_RET).astype(jnp.int32)) * float(C))

    states = [jnp.zeros((LANES, LANES), F32)]

    def chunk(c):
        first = lax.broadcasted_iota(jnp.int32, (1, LANES), 1) < DK_RET
        off = c * C
        q = q_ref[pl.ds(off, C), :]
        k = k_ref[pl.ds(off, C), :]
        v = v_ref[pl.ds(off, C), :]
        zero = jnp.zeros_like(q)
        qs = jnp.concatenate([jnp.where(first, q, zero), jnp.where(first, zero, q)], axis=0)
        s = lax.dot_general(qs, k.astype(BF16), (((1,), (1,)), ((), ())), preferred_element_type=F32)
        upd = lax.dot_general((k * k_dec).astype(BF16), v, (((0,), (0,)), ((), ())),
                              preferred_element_type=F32)
        states.append(state_dec * states[c] + same_head * upd)
        yield
        pv = jnp.dot((s * decay).astype(BF16), v, preferred_element_type=F32)
        cross = jnp.dot(q, states[c].astype(BF16), preferred_element_type=F32) * cross_dec
        yield
        o = jnp.where(first, pv[:C], pv[C:]) + cross
        yield
        s_a = jnp.sum(jnp.where(first, o, 0.0), axis=-1, keepdims=True)
        s_b = jnp.sum(jnp.where(first, 0.0, o), axis=-1, keepdims=True)
        xc = o - jnp.where(first, s_a, s_b) * (1.0 / DK_RET)
        sq = xc * xc
        yield
        v_a = jnp.sum(jnp.where(first, sq, 0.0), axis=-1, keepdims=True)
        v_b = jnp.sum(jnp.where(first, 0.0, sq), axis=-1, keepdims=True)
        rn = xc * lax.rsqrt(jnp.where(first, v_a, v_b) * (1.0 / DK_RET) + LN_EPS)
        g = g_ref[pl.ds(off, C), :]
        r_ref[pl.ds(off, C), :] = (rn * (g * _sigmoid(g))).astype(r_ref.dtype)

    yield from _skewed([chunk(c) for c in range(n_chunks)], 5)
    st_ref[0, 0] = states[n_chunks][:DK_RET, :DK_RET]
    st_ref[0, 1] = states[n_chunks][DK_RET:, DK_RET:]


def _needs_pitch(dil):
    return dil % 8 == 0


def _att_body(q_ref, k_ref, v_ref, o_ref, num_ref, den_ref, max_ref, qp_ref, kp_ref, vp_ref, seq):
    QB = Q_BLOCK
    pitched = [dil for _, dil in DILATIONS if _needs_pitch(dil)]
    assert len(pitched) <= 1
    for dil in pitched:
        for g in range(seq // dil):
            for src, dst in ((q_ref, qp_ref), (k_ref, kp_ref), (v_ref, vp_ref)):
                dst[g * (dil + 1):g * (dil + 1) + dil, :] = src[g * dil:(g + 1) * dil, :]
    ri = lax.broadcasted_iota(jnp.int32, (2 * QB, 2 * QB), 0)
    ci = lax.broadcasted_iota(jnp.int32, (2 * QB, 2 * QB), 1)
    qi = jnp.where(ri >= QB, ri - QB, ri)
    valid_band = ((ci < QB) & (ci >= qi)) | ((ci >= QB) & (ci - QB <= qi))
    bias_band = jnp.where(valid_band, 0.0, NEG_BIG)
    rd = lax.broadcasted_iota(jnp.int32, (2 * QB, QB), 0)
    cd = lax.broadcasted_iota(jnp.int32, (2 * QB, QB), 1)
    bias_diag = jnp.where(cd <= jnp.where(rd >= QB, rd - QB, rd), 0.0, NEG_BIG)
    ones = jnp.ones((2 * QB, LANES), BF16)

    def unit(br, dil, q_start, k_start, n_keys, bias):
        lane = lax.broadcasted_iota(jnp.int32, (1, LANES), 1)
        first = lane < HD_ATT
        rows_q = pl.ds(q_start, QB, stride=dil) if dil > 1 else pl.ds(q_start, QB)
        if _needs_pitch(dil):
            src_q, src_k, src_v = qp_ref, kp_ref, vp_ref
            rows_qs = pl.ds(q_start, QB, stride=dil + 1)
            rows_k = pl.ds(k_start, n_keys, stride=dil + 1)
        else:
            src_q, src_k, src_v = q_ref, k_ref, v_ref
            rows_qs = rows_q
            rows_k = pl.ds(k_start, n_keys, stride=dil) if dil > 1 else pl.ds(k_start, n_keys)
        q = src_q[rows_qs, :] * ATT_SCALE
        qs = jnp.concatenate([jnp.where(first, q, 0.0), jnp.where(first, 0.0, q)], axis=0).astype(BF16)
        kk = src_k[rows_k, :].astype(BF16)
        s = lax.dot_general(qs, kk, (((1,), (1,)), ((), ())), preferred_element_type=F32)
        yield
        s = s + bias
        m = jnp.max(s, axis=-1, keepdims=True)
        yield
        p = jnp.exp(s - m).astype(BF16)
        yield
        vv = src_v[rows_k, :].astype(BF16)
        pv = jnp.dot(p, jnp.concatenate([vv, ones[:n_keys]], axis=1), preferred_element_type=F32)
        yield
        mb = jnp.broadcast_to(m, (2 * QB, LANES))
        num_ref[br, rows_q, :] = jnp.where(first, pv[:QB, :LANES], pv[QB:, :LANES])
        den_ref[br, rows_q, :] = jnp.where(first, pv[:QB, LANES:], pv[QB:, LANES:])
        max_ref[br, rows_q, :] = jnp.where(first, mb[:QB], mb[QB:])

    units, last_unit_of_chunk = [], {}
    for br, (window, dil) in sorted(enumerate(DILATIONS), key=lambda e: -e[1][1]):
        assert window // dil == QB
        sub_len = seq // dil
        n_blocks = sub_len // QB
        assert not (_needs_pitch(dil) and n_blocks > 1)
        for r in range(dil):
            for n in range(n_blocks):
                q_start = r + dil * n * QB
                for row in range(q_start, q_start + dil * QB, dil):
                    last_unit_of_chunk[row // COMBINE_ROWS] = len(units)
                if n == 0:
                    units.append(unit(br, dil, q_start, r, QB, bias_diag))
                else:
                    units.append(unit(br, dil, q_start, q_start - dil * QB, 2 * QB, bias_band))

    def combine(i):
        rows = pl.ds(i * COMBINE_ROWS, COMBINE_ROWS)
        m0, m1, m2 = max_ref[0, rows, :], max_ref[1, rows, :], max_ref[2, rows, :]
        mx = jnp.maximum(jnp.maximum(m0, m1), m2)
        e0, e1, e2 = jnp.exp(m0 - mx), jnp.exp(m1 - mx), jnp.exp(m2 - mx)
        num = e0 * num_ref[0, rows, :] + e1 * num_ref[1, rows, :] + e2 * num_ref[2, rows, :]
        den = e0 * den_ref[0, rows, :] + e1 * den_ref[1, rows, :] + e2 * den_ref[2, rows, :]
        o_ref[rows, :] = (num / den).astype(o_ref.dtype)

    n_stages = 5
    pending = sorted(range(seq // COMBINE_ROWS), key=lambda i: last_unit_of_chunk[i])
    step = 0
    for _ in _skewed(units, n_stages):
        yield
        while pending and last_unit_of_chunk[pending[0]] + n_stages - 1 <= step:
            combine(pending.pop(0))
            yield
        step += 1
    assert not pending


ATT_STEPS_PER_RET_CHUNK = 1


def _mixer_kernel(qr_ref, kr_ref, vr_ref, gr_ref, qa_ref, ka_ref, va_ref, r_ref, st_ref, o_ref,
                  num_ref, den_ref, max_ref, qp_ref, kp_ref, vp_ref, *, seq):
    for _ in _round_robin([
            (_ret_body(qr_ref, kr_ref, vr_ref, gr_ref, r_ref, st_ref, pl.program_id(1), seq // RET_CHUNK), 1),
            (_att_body(qa_ref, ka_ref, va_ref, o_ref, num_ref, den_ref, max_ref, qp_ref, kp_ref, vp_ref, seq),
             ATT_STEPS_PER_RET_CHUNK)]):
        pass


def _mixer_call(qr, kr, vr, gr, qa, ka, va, batch, seq):
    assert H_RET == H_ATT and seq % RET_CHUNK == 0 and seq % COMBINE_ROWS == 0
    blk = pl.BlockSpec((seq, LANES), lambda b, p: (b, p))
    n_br = len(DILATIONS)
    pitched_rows = max([seq // dil * (dil + 1) for _, dil in DILATIONS if _needs_pitch(dil)] + [8])
    return pl.pallas_call(
        functools.partial(_mixer_kernel, seq=seq),
        grid=(batch, H_RET // HEADS_PER_TILE),
        in_specs=[blk] * 7,
        out_specs=[blk, pl.BlockSpec((1, HEADS_PER_TILE, DK_RET, DK_RET), lambda b, p: (b, p, 0, 0)), blk],
        out_shape=[jax.ShapeDtypeStruct((batch * seq, RET_WIDTH), BF16),
                   jax.ShapeDtypeStruct((batch, H_RET, DK_RET, DK_RET), F32),
                   jax.ShapeDtypeStruct((batch * seq, ATT_WIDTH), BF16)],
        scratch_shapes=[pltpu.VMEM((n_br, seq, LANES), F32)] * 3 + [pltpu.VMEM((pitched_rows, LANES), F32)] * 3,
        compiler_params=pltpu.CompilerParams(dimension_semantics=("parallel", "parallel"),
                                             vmem_limit_bytes=VMEM_LIMIT),
    )(qr, kr, vr, gr, qa, ka, va)


def _post_kernel(*refs, tm, tiles_per_seq, shift):
    (x_ref, r_ref, a_ref, p_ref, lng_ref, lnb_ref, wo_ref, g1_ref, b1_ref, wup_ref, cw_ref, cb_ref,
     wdn_ref, wpg_ref, wpp_ref, g2_ref, b2_ref) = refs[:17]
    if shift:
        y_ref, cs_ref, h1b_s, halo_s = refs[17:]
    else:
        pre0_ref, pre1_ref, y_ref, cs_ref, h1b_s = refs[17:]

    h = _layer_norm(x_ref[...], lng_ref[...], lnb_ref[...])
    mix = (jnp.dot(r_ref[...].astype(BF16), wo_ref[:RET_WIDTH, :], preferred_element_type=F32)
           + jnp.dot(a_ref[...].astype(BF16), wo_ref[RET_WIDTH:, :], preferred_element_type=F32))
    h1 = _layer_norm(ALPHA * h + mix, g1_ref[...], b1_ref[...])
    h1b_s[...] = h1.astype(BF16)
    gate = _sigmoid(jnp.dot(h1b_s[...], wpg_ref[...], preferred_element_type=F32))
    y_ref[...] = ALPHA * h1 + gate * jnp.dot(p_ref[...].astype(BF16), wpp_ref[...], preferred_element_type=F32)

    if shift:
        @pl.when(pl.program_id(0) % tiles_per_seq == 0)
        def _():
            halo_s[...] = jnp.zeros_like(halo_s)
        row8 = lax.broadcasted_iota(jnp.int32, (8, FF_CHUNK), 0)

    n_chunks = D_FF // FF_CHUNK
    partials, acts = [], []
    cols_of = lambda part, j: slice(part * D_FF + j * FF_CHUNK, part * D_FF + (j + 1) * FF_CHUNK)
    kw = D_MODEL // UP_K_SPLIT
    grp = min(tm, ACT_GROUP_ROWS)
    u_of = {}

    def up_gen(j):
        for part in (0, 1):
            acc = None
            for kq in range(UP_K_SPLIT):
                d = jnp.dot(h1b_s[:, kq * kw:(kq + 1) * kw], wup_ref[kq * kw:(kq + 1) * kw, cols_of(part, j)],
                            preferred_element_type=F32)
                acc = d if acc is None else acc + d
                yield
            u_of[j, part] = acc

    def act_gen(j, out):
        pieces = []
        for r0 in range(0, tm, grp):
            conv = []
            for part in (0, 1):
                cols = cols_of(part, j)
                u = u_of[j, part][r0:r0 + grp]
                if shift:
                    above = halo_s[:, cols] if r0 == 0 else u_of[j, part][r0 - 8:r0]

                    def shifted(k, u=u, above=above):
                        rolled = pltpu.roll(u, k, 0)
                        top = jnp.where(row8 < k, pltpu.roll(above, k, 0), rolled[0:8])
                        return jnp.concatenate([top, rolled[8:]], axis=0)

                    prev2, prev1 = shifted(2), shifted(1)
                    if r0 + grp == tm:
                        halo_s[:, cols] = u[grp - 8:, :]
                        cs_ref[0, :, cols] = u[grp - (CONV_W - 1):, :]
                else:
                    cs_ref[r0:r0 + grp, cols] = u
                    prev2 = pre0_ref[r0:r0 + grp, cols]
                    prev1 = pre1_ref[r0:r0 + grp, cols]
                conv.append(cb_ref[:, cols] + cw_ref[0:1, cols] * prev2 + cw_ref[1:2, cols] * prev1
                            + cw_ref[2:3, cols] * u)
                yield
            pieces.append((conv[0] * _gelu_tanh(conv[1])).astype(BF16))
            yield
        del u_of[j, 0], u_of[j, 1]
        out.append(jnp.concatenate(pieces, axis=0))

    for j in range(n_chunks + 1):
        for _ in _round_robin(([(up_gen(j), 1)] if j < n_chunks else [])
                              + ([(act_gen(j - 1, acts), ACT_STEPS_PER_UP_STEP)] if j else [])):
            pass
        if j and (len(acts) == DOWN_GROUP or j == n_chunks):
            r0 = (j - len(acts)) * FF_CHUNK
            partials.append(jnp.dot(jnp.concatenate(acts, axis=1), wdn_ref[r0:j * FF_CHUNK, :],
                                    preferred_element_type=F32))
            acts = []

    y_ref[...] = _layer_norm(y_ref[...] + sum(partials[1:], partials[0]), g2_ref[...], b2_ref[...])


def _post_call(x2d, r2d, a2d, p2d, weights, tm, tiles_per_seq, prefix=None):
    n = x2d.shape[0]
    shift = prefix is None
    two_f = 2 * D_FF
    row = lambda i: (i, 0)
    in_specs = [pl.BlockSpec((tm, D_MODEL), row), pl.BlockSpec((tm, RET_WIDTH), row),
                pl.BlockSpec((tm, ATT_WIDTH), row), pl.BlockSpec((tm, PLE_DIM), row)]
    in_specs += [_const_spec(w.shape) for w in weights]
    args = [x2d, r2d, a2d, p2d, *weights]
    scratch = [pltpu.VMEM((tm, D_MODEL), BF16)]
    if shift:
        out_specs = [pl.BlockSpec((tm, D_MODEL), row),
                     pl.BlockSpec((1, CONV_W - 1, two_f), lambda i: (i // tiles_per_seq, 0, 0))]
        out_shape = [jax.ShapeDtypeStruct((n, D_MODEL), F32),
                     jax.ShapeDtypeStruct((n // (tm * tiles_per_seq), CONV_W - 1, two_f), F32)]
        scratch += [pltpu.VMEM((8, two_f), F32)]
    else:
        in_specs += [pl.BlockSpec((tm, two_f), row)] * 2
        args += list(prefix)
        out_specs = [pl.BlockSpec((tm, D_MODEL), row), pl.BlockSpec((tm, two_f), row)]
        out_shape = [jax.ShapeDtypeStruct((n, D_MODEL), F32), jax.ShapeDtypeStruct((n, two_f), F32)]
    return pl.pallas_call(
        functools.partial(_post_kernel, tm=tm, tiles_per_seq=tiles_per_seq, shift=shift),
        grid=(n // tm,),
        in_specs=in_specs,
        out_specs=out_specs,
        out_shape=out_shape,
        scratch_shapes=scratch,
        compiler_params=pltpu.CompilerParams(dimension_semantics=("arbitrary",), vmem_limit_bytes=VMEM_LIMIT),
    )(*args)


def _sample_mix_kernel(qr_ref, kr_ref, vr_ref, gr_ref, st_ref, qa_ref, ka_ref, qat_ref, vat_ref,
                       kt_ref, vt_ref, r_ref, att_ref, nst_ref, s_s, pc_s):
    hrow = lax.broadcasted_iota(jnp.int32, (H_RET, RET_WIDTH), 0)
    hlane = lax.broadcasted_iota(jnp.int32, (H_RET, RET_WIDTH), 1) // DK_RET
    own = hrow == hlane

    qm = jnp.where(own, qr_ref[0], 0.0)
    km = jnp.where(own, kr_ref[0], 0.0)
    v8 = vr_ref[0]
    g8 = gr_ref[0]
    st = st_ref[0]
    lg8 = _select_log_gamma(lax.broadcasted_iota(jnp.int32, (H_RET, 1), 0))
    cross = jnp.dot(qm.astype(BF16), st.astype(BF16), preferred_element_type=F32) * jnp.exp(lg8)
    qk = jnp.sum(qm * km, axis=-1, keepdims=True)
    o = qk * v8 + cross
    lg_rows = _select_log_gamma(lax.broadcasted_iota(jnp.int32, (H_RET * DK_RET, 1), 0) // DK_RET)
    outer = lax.dot_general(km, v8, (((0,), (0,)), ((), ())), preferred_element_type=F32,
                            precision=lax.Precision.HIGHEST)
    nst_ref[0] = jnp.exp(lg_rows) * st + outer
    mu = jnp.mean(o, axis=-1, keepdims=True)
    xc = o - mu
    var = jnp.mean(xc * xc, axis=-1, keepdims=True)
    r_ref[0] = xc * lax.rsqrt(var + LN_EPS) * (g8 * _sigmoid(g8))

    n_past = kt_ref.shape[-1]
    s_new = jnp.sum(qa_ref[0] * ka_ref[0], axis=-1, keepdims=True) * ATT_SCALE
    qt = qat_ref[0] * ATT_SCALE
    for h in range(H_ATT):
        s_s[h:h + 1, :] = jnp.sum(kt_ref[h] * qt[:, h:h + 1], axis=0, keepdims=True)
    tok = lax.broadcasted_iota(jnp.int32, (H_ATT, n_past), 1)
    stats = []
    for window, dil in DILATIONS:
        lo = n_past - window
        sb = s_s[:, lo:] + jnp.where(tok[:, lo:] % dil == n_past % dil, 0.0, NEG_BIG)
        m = jnp.maximum(jnp.max(sb, axis=-1, keepdims=True), s_new)
        p = jnp.exp(sb - m)
        p_new = jnp.exp(s_new - m)
        den = jnp.sum(p, axis=-1, keepdims=True) + p_new
        stats.append((lo, p, p_new, den, m + jnp.log(den)))
    mx = jnp.maximum(jnp.maximum(stats[0][4], stats[1][4]), stats[2][4])
    es = [jnp.exp(st_[4] - mx) for st_ in stats]
    tot = es[0] + es[1] + es[2]
    coef = [e / (tot * st_[3]) for e, st_ in zip(es, stats)]
    w_new = coef[0] * stats[0][2] + coef[1] * stats[1][2] + coef[2] * stats[2][2]
    order = sorted(range(len(stats)), key=lambda i: stats[i][0])
    assert stats[order[0]][0] == 0
    pc_s[...] = coef[order[0]] * stats[order[0]][1]
    for i in order[1:]:
        lo = stats[i][0]
        pc_s[:, lo:] = pc_s[:, lo:] + coef[i] * stats[i][1]
    vt_new = vat_ref[0]
    for h in range(H_ATT):
        col = jnp.sum(vt_ref[h] * pc_s[h:h + 1, :], axis=-1, keepdims=True)
        att_ref[0, :, h:h + 1] = col + w_new[h:h + 1, :] * vt_new[:, h:h + 1]


def _sample_mix_call(qr, kr, vr, gr, state, qa, ka, va, cache_k, cache_v):
    nb = qr.shape[0]
    n_past = cache_k.shape[-1]
    assert all(w <= n_past and n_past % d == 0 for w, d in DILATIONS) and max(w for w, _ in DILATIONS) == n_past
    row3 = pl.BlockSpec((1, 1, RET_WIDTH), lambda b: (b, 0, 0))
    head3 = pl.BlockSpec((1, H_RET, DK_RET), lambda b: (b, 0, 0))
    col3 = pl.BlockSpec((1, HD_ATT, H_ATT), lambda b: (b, 0, 0))
    st_spec = pl.BlockSpec((1, H_RET * DK_RET, DK_RET), lambda b: (b, 0, 0))
    cache_spec = pl.BlockSpec((None, H_ATT, HD_ATT, n_past), lambda b: (b, 0, 0, 0))
    r3 = lambda a: a.reshape(nb, 1, RET_WIDTH)
    h3 = lambda a: a.reshape(nb, H_RET, DK_RET)
    t3 = lambda a: jnp.swapaxes(h3(a), 1, 2)
    r_s, att_t, nst = pl.pallas_call(
        _sample_mix_kernel,
        grid=(nb,),
        in_specs=[row3, row3, head3, head3, st_spec, head3, head3, col3, col3, cache_spec, cache_spec],
        out_specs=[head3, col3, st_spec],
        out_shape=[jax.ShapeDtypeStruct((nb, H_RET, DK_RET), F32),
                   jax.ShapeDtypeStruct((nb, HD_ATT, H_ATT), F32),
                   jax.ShapeDtypeStruct((nb, H_RET * DK_RET, DK_RET), F32)],
        scratch_shapes=[pltpu.VMEM((H_ATT, n_past), F32), pltpu.VMEM((H_ATT, n_past), F32)],
        compiler_params=pltpu.CompilerParams(dimension_semantics=("parallel",), vmem_limit_bytes=VMEM_LIMIT),
    )(r3(qr), r3(kr), h3(vr), h3(gr), state.reshape(nb, H_RET * DK_RET, DK_RET), h3(qa), h3(ka), t3(qa), t3(va),
      cache_k, cache_v)
    return r_s, jnp.swapaxes(att_t, 1, 2), nst


def kernel(x_prompt, x_sample, cache_k_win, cache_v_win, state_ret, state_conv, p_prompt, p_sample,
           ln_in_g, ln_in_b, w_in, w_out, ln1_g, ln1_b, w_up, conv_w, conv_b, w_down,
           w_ple_gate, w_ple_proj, ln2_g, ln2_b):
    B, S, _ = x_prompt.shape
    NB, T, _ = x_sample.shape
    assert T == 1 and w_in.shape[0] == DEPTH == 1 and S % PROJ_TILE == 0 and S % POST_TILE == 0
    two_f = 2 * D_FF
    vec = lambda a: a.reshape(1, -1)
    w_in_b = w_in[0].astype(BF16)
    post_w = (vec(ln_in_g), vec(ln_in_b), w_out[0].astype(BF16), vec(ln1_g[0]), vec(ln1_b[0]),
              w_up[0].astype(BF16), conv_w[0], vec(conv_b[0]), w_down[0].astype(BF16),
              w_ple_gate[0].astype(BF16), w_ple_proj[0].astype(BF16), vec(ln2_g[0]), vec(ln2_b[0]))

    pos_p = jnp.arange(S, dtype=jnp.int32)
    tabs_p = _rot_tables(pos_p, DK_RET, RET_ROPE_BASE) + _rot_tables(pos_p, ROT_DIMS, ATT_ROPE_THETA)
    xp = x_prompt.reshape(B * S, D_MODEL)
    qr, kr, vr, gr, qa, ka, va, ka_t, va_t = _proj_call(
        xp, vec(ln_in_g), vec(ln_in_b), w_in_b, tabs_p, PROJ_TILE, S // PROJ_TILE,
        (BF16, F32, BF16, F32, F32, F32, F32), seq_for_transposed=S)
    r_p, ret_fin, att_p = _mixer_call(qr, kr, vr, gr, qa, ka, va, B, S)
    y_p, conv_p = _post_call(xp, r_p, att_p, p_prompt[0].reshape(B * S, PLE_DIM), post_w,
                             POST_TILE, S // POST_TILE)
    keep = min(WINDOW_MAX, S)
    k_win_p = jnp.transpose(ka_t, (0, 3, 1, 2))[:, S - keep:]
    v_win_p = jnp.transpose(va_t, (0, 3, 1, 2))[:, S - keep:]

    pos_s = jnp.full((NB,), PAST_LEN, jnp.int32)
    tabs_s = _rot_tables(pos_s, DK_RET, RET_ROPE_BASE) + _rot_tables(pos_s, ROT_DIMS, ATT_ROPE_THETA)
    xs = x_sample.reshape(NB, D_MODEL)
    sqr, skr, svr, sgr, sqa, ska, sva = _proj_call(xs, vec(ln_in_g), vec(ln_in_b), w_in_b, tabs_s, NB, 1,
                                                   (F32,) * 7)
    r_s, att_s, nst = _sample_mix_call(sqr, skr, svr, sgr, state_ret[0], sqa, ska, sva,
                                       jnp.transpose(cache_k_win[0], (0, 2, 3, 1)),
                                       jnp.transpose(cache_v_win[0], (0, 2, 3, 1)))
    y_s, u_s = _post_call(xs, r_s.reshape(NB, RET_WIDTH), att_s.reshape(NB, ATT_WIDTH), p_sample[0].reshape(NB, PLE_DIM),
                          post_w, NB, 1, prefix=(state_conv[0][:, 0], state_conv[0][:, 1]))
    conv_s = jnp.stack([state_conv[0][:, 1], u_s], axis=1)

    return (y_p.reshape(B, S, D_MODEL), y_s.reshape(NB, 1, D_MODEL),
            k_win_p[None], v_win_p[None], ret_fin[None], conv_p[None],
            ska.reshape(1, NB, 1, H_ATT, HD_ATT), sva.reshape(1, NB, 1, H_ATT, HD_ATT),
            nst.reshape(1, NB, H_RET, DK_RET, DK_RET), conv_s[None])
```

```python
import functools
import math

import numpy as np
import jax
import jax.numpy as jnp
from jax import lax
from jax.experimental import pallas as pl
from jax.experimental.pallas import tpu as pltpu

F32 = jnp.float32
BF16 = jnp.bfloat16

D_MODEL = 1024
PAST_LEN = 16384
H_RET = 8
DK_RET = 64
RET_WIDTH = 512
RET_CHUNK = 128
RET_ROPE_BASE = 10000.0
H_ATT = 8
HD_ATT = 64
ATT_WIDTH = 512
ATT_ROPE_THETA = 500000.0
ROT_DIMS = HD_ATT // 4
DILATIONS = ((128, 1), (512, 4), (2048, 16))
WINDOW_MAX = 2048
Q_BLOCK = 128
PROJ_WIDTH = 4 * RET_WIDTH + 3 * ATT_WIDTH
D_FF = 2816
CONV_W = 3
PLE_DIM = 256
LN_EPS = 1e-5
DEPTH = 1
ALPHA = (2 * DEPTH) ** 0.25

LANES = 128
HEADS_PER_TILE = LANES // DK_RET
NEG_BIG = -1e30
LOG_GAMMA = tuple(math.log(1.0 - 2.0 ** (-5.0 - h)) for h in range(H_RET))
ATT_SCALE = HD_ATT ** -0.5
VMEM_LIMIT = 56 * 1024 * 1024

PROJ_TILE = 512
POST_TILE = 512
FF_CHUNK = 256
DOWN_GROUP = 3
UP_K_SPLIT = 2
ACT_GROUP_ROWS = 128
ACT_STEPS_PER_UP_STEP = 3


def _layer_norm(x, g, b):
    mu = jnp.mean(x, axis=-1, keepdims=True)
    xc = x - mu
    var = jnp.mean(xc * xc, axis=-1, keepdims=True)
    return xc * lax.rsqrt(var + LN_EPS) * g + b


def _sigmoid(x):
    return 1.0 / (1.0 + jnp.exp(-x))


def _gelu_tanh(x):
    c0 = math.sqrt(2.0 / math.pi)
    return x * (0.5 + 0.5 * jnp.tanh(x * (c0 + (c0 * 0.044715) * (x * x))))


def _select_log_gamma(head_idx):
    out = jnp.zeros(head_idx.shape, F32)
    for h in range(H_RET):
        out = jnp.where(head_idx == h, LOG_GAMMA[h], out)
    return out


def _const_spec(shape):
    nd = len(shape)
    return pl.BlockSpec(shape, lambda *_: (0,) * nd, pipeline_mode=pl.Buffered(1))


def _rotate(z, cos, s_lo, s_hi, half):
    up = pltpu.roll(z, LANES - half, 1)
    dn = pltpu.roll(z, half, 1)
    return z * cos + up * s_lo + dn * s_hi


PROJ_GROUP_ORDER = (5, 6, 0, 1, 4, 3, 2)


def _proj_kernel(x_ref, g_ref, b_ref, w_ref, cr_ref, slr_ref, shr_ref, ca_ref, sla_ref, sha_ref,
                 qr_ref, kr_ref, vr_ref, gr_ref, qa_ref, ka_ref, va_ref, *t_refs):
    outs = (qr_ref, kr_ref, vr_ref, gr_ref, qa_ref, ka_ref, va_ref)
    tm = x_ref.shape[0]
    cols = lambda grp: slice(grp * RET_WIDTH, (grp + 1) * RET_WIDTH)
    part = tm // 2 if tm % 32 == 0 else tm
    hb_parts = [_layer_norm(x_ref[r0:r0 + part], g_ref[...], b_ref[...]).astype(BF16) for r0 in range(0, tm, part)]
    z_next = jnp.concatenate([jnp.dot(hp, w_ref[:, cols(PROJ_GROUP_ORDER[0])], preferred_element_type=F32)
                              for hp in hb_parts], axis=0)
    hb = jnp.concatenate(hb_parts, axis=0)
    for i, grp in enumerate(PROJ_GROUP_ORDER):
        o_ref = outs[grp]
        zg = z_next
        if i + 1 < len(PROJ_GROUP_ORDER):
            z_next = jnp.dot(hb, w_ref[:, cols(PROJ_GROUP_ORDER[i + 1])], preferred_element_type=F32)
        for j in range(RET_WIDTH // LANES):
            z = zg[:, j * LANES:(j + 1) * LANES]
            if grp in (0, 1):
                z = _rotate(z, cr_ref[...], slr_ref[...], shr_ref[...], DK_RET // 2)
                if grp == 1:
                    z = z * (DK_RET ** -0.5)
            elif grp in (4, 5):
                z = _rotate(z, ca_ref[...], sla_ref[...], sha_ref[...], ROT_DIMS // 2)
            o_ref[:, j * LANES:(j + 1) * LANES] = z.astype(o_ref.dtype)
            if t_refs and grp in (5, 6):
                zt = z.T
                for hh in range(HEADS_PER_TILE):
                    t_refs[grp - 5][0, HEADS_PER_TILE * j + hh] = zt[hh * HD_ATT:(hh + 1) * HD_ATT]


def _rot_tables(pos, n_rot, base):
    half = n_rot // 2
    inv = base ** (-jnp.arange(half, dtype=F32) / half)
    ang = pos.astype(F32)[:, None] * inv[None, :]
    cos, sin = jnp.cos(ang), jnp.sin(ang)
    l = np.arange(LANES) % DK_RET
    idx = np.where(l < half, l, np.where(l < n_rot, l - half, 0))
    lo = jnp.asarray(l < half)
    hi = jnp.asarray((l >= half) & (l < n_rot))
    cos_f = jnp.where(jnp.asarray(l < n_rot), cos[:, idx], 1.0)
    s_lo = jnp.where(lo, -sin[:, idx], 0.0)
    s_hi = jnp.where(hi, sin[:, idx], 0.0)
    return cos_f, s_lo, s_hi


def _proj_call(x2d, ln_g, ln_b, w_in_b, tabs, tm, tab_blocks, out_dtypes, seq_for_transposed=None):
    n = x2d.shape[0]
    row = lambda i: (i, 0)
    tab = lambda i: (i % tab_blocks, 0)
    tab_spec = pl.BlockSpec((tm, LANES), tab)
    out_specs = [pl.BlockSpec((tm, RET_WIDTH), row)] * 7
    out_shape = [jax.ShapeDtypeStruct((n, RET_WIDTH), dt) for dt in out_dtypes]
    if seq_for_transposed is not None:
        tps = seq_for_transposed // tm
        out_specs += [pl.BlockSpec((1, H_ATT, HD_ATT, tm), lambda i: (i // tps, 0, 0, i % tps))] * 2
        out_shape += [jax.ShapeDtypeStruct((n // seq_for_transposed, H_ATT, HD_ATT, seq_for_transposed), F32)] * 2
    return pl.pallas_call(
        _proj_kernel,
        grid=(n // tm,),
        in_specs=[pl.BlockSpec((tm, D_MODEL), row), _const_spec((1, D_MODEL)), _const_spec((1, D_MODEL)),
                  _const_spec((D_MODEL, PROJ_WIDTH))] + [tab_spec] * 6,
        out_specs=out_specs,
        out_shape=out_shape,
        compiler_params=pltpu.CompilerParams(dimension_semantics=("parallel",), vmem_limit_bytes=VMEM_LIMIT),
    )(x2d, ln_g, ln_b, w_in_b, *tabs)


COMBINE_ROWS = 256


def _round_robin(entries):
    live = list(entries)
    while live:
        for entry in list(live):
            for _ in range(entry[1]):
                if next(entry[0], live) is live:
                    live.remove(entry)
                    break
                yield


def _skewed(items, n_stages):
    for t in range(len(items) + n_stages - 1):
        for stage in range(n_stages):
            if 0 <= t - stage < len(items):
                next(items[t - stage], None)
        yield


def _ret_body(q_ref, k_ref, v_ref, g_ref, r_ref, st_ref, hp, n_chunks):
    C = RET_CHUNK
    lane1 = lax.broadcasted_iota(jnp.int32, (1, LANES), 1)
    lg_lane = _select_log_gamma(HEADS_PER_TILE * hp + (lane1 >= DK_RET).astype(jnp.int32))

    ri = lax.broadcasted_iota(jnp.int32, (2 * C, C), 0)
    ci = lax.broadcasted_iota(jnp.int32, (2 * C, C), 1)
    rel = (jnp.where(ri >= C, ri - C, ri) - ci).astype(F32)
    lg_rows = _select_log_gamma(HEADS_PER_TILE * hp + (ri >= C).astype(jnp.int32))
    decay = jnp.where(rel >= 0, jnp.exp(lg_rows * jnp.maximum(rel, 0.0)), 0.0)

    tok = lax.broadcasted_iota(jnp.int32, (C, LANES), 0).astype(F32)
    cross_dec = jnp.exp(lg_lane * (tok + 1.0))
    k_dec = jnp.exp(lg_lane * (C - 1.0 - tok))
    sr = lax.broadcasted_iota(jnp.int32, (LANES, LANES), 0)
    sc = lax.broadcasted_iota(jnp.int32, (LANES, LANES), 1)
    same_head = ((sr >= DK_RET) == (sc >= DK_RET)).astype(F32)
    state_dec = jnp.exp(_select_log_gamma(HEADS_PER_TILE * hp + (sr >= DK_RET).astype(jnp.int32)) * float(C))

    states = [jnp.zeros((LANES, LANES), F32)]

    def chunk(c):
        first = lax.broadcasted_iota(jnp.int32, (1, LANES), 1) < DK_RET
        off = c * C
        q = q_ref[pl.ds(off, C), :]
        k = k_ref[pl.ds(off, C), :]
        v = v_ref[pl.ds(off, C), :]
        zero = jnp.zeros_like(q)
        qs = jnp.concatenate([jnp.where(first, q, zero), jnp.where(first, zero, q)], axis=0)
        s = lax.dot_general(qs, k.astype(BF16), (((1,), (1,)), ((), ())), preferred_element_type=F32)
        upd = lax.dot_general((k * k_dec).astype(BF16), v, (((0,), (0,)), ((), ())),
                              preferred_element_type=F32)
        states.append(state_dec * states[c] + same_head * upd)
        yield
        pv = jnp.dot((s * decay).astype(BF16), v, preferred_element_type=F32)
        cross = jnp.dot(q, states[c].astype(BF16), preferred_element_type=F32) * cross_dec
        yield
        o = jnp.where(first, pv[:C], pv[C:]) + cross
        yield
        s_a = jnp.sum(jnp.where(first, o, 0.0), axis=-1, keepdims=True)
        s_b = jnp.sum(jnp.where(first, 0.0, o), axis=-1, keepdims=True)
        xc = o - jnp.where(first, s_a, s_b) * (1.0 / DK_RET)
        sq = xc * xc
        yield
        v_a = jnp.sum(jnp.where(first, sq, 0.0), axis=-1, keepdims=True)
        v_b = jnp.sum(jnp.where(first, 0.0, sq), axis=-1, keepdims=True)
        rn = xc * lax.rsqrt(jnp.where(first, v_a, v_b) * (1.0 / DK_RET) + LN_EPS)
        g = g_ref[pl.ds(off, C), :]
        r_ref[pl.ds(off, C), :] = (rn * (g * _sigmoid(g))).astype(r_ref.dtype)

    yield from _skewed([chunk(c) for c in range(n_chunks)], 5)
    st_ref[0, 0] = states[n_chunks][:DK_RET, :DK_RET]
    st_ref[0, 1] = states[n_chunks][DK_RET:, DK_RET:]


def _needs_pitch(dil):
    return dil % 8 == 0


def _att_body(q_ref, k_ref, v_ref, o_ref, num_ref, den_ref, max_ref, qp_ref, kp_ref, vp_ref, seq):
    QB = Q_BLOCK
    pitched = [dil for _, dil in DILATIONS if _needs_pitch(dil)]
    assert len(pitched) <= 1
    for dil in pitched:
        for g in range(seq // dil):
            for src, dst in ((q_ref, qp_ref), (k_ref, kp_ref), (v_ref, vp_ref)):
                dst[g * (dil + 1):g * (dil + 1) + dil, :] = src[g * dil:(g + 1) * dil, :]
    ri = lax.broadcasted_iota(jnp.int32, (2 * QB, 2 * QB), 0)
    ci = lax.broadcasted_iota(jnp.int32, (2 * QB, 2 * QB), 1)
    qi = jnp.where(ri >= QB, ri - QB, ri)
    valid_band = ((ci < QB) & (ci >= qi)) | ((ci >= QB) & (ci - QB <= qi))
    bias_band = jnp.where(valid_band, 0.0, NEG_BIG)
    rd = lax.broadcasted_iota(jnp.int32, (2 * QB, QB), 0)
    cd = lax.broadcasted_iota(jnp.int32, (2 * QB, QB), 1)
    bias_diag = jnp.where(cd <= jnp.where(rd >= QB, rd - QB, rd), 0.0, NEG_BIG)
    ones = jnp.ones((2 * QB, LANES), BF16)

    def unit(br, dil, q_start, k_start, n_keys, bias):
        lane = lax.broadcasted_iota(jnp.int32, (1, LANES), 1)
        first = lane < HD_ATT
        rows_q = pl.ds(q_start, QB, stride=dil) if dil > 1 else pl.ds(q_start, QB)
        if _needs_pitch(dil):
            src_q, src_k, src_v = qp_ref, kp_ref, vp_ref
            rows_qs = pl.ds(q_start, QB, stride=dil + 1)
            rows_k = pl.ds(k_start, n_keys, stride=dil + 1)
        else:
            src_q, src_k, src_v = q_ref, k_ref, v_ref
            rows_qs = rows_q
            rows_k = pl.ds(k_start, n_keys, stride=dil) if dil > 1 else pl.ds(k_start, n_keys)
        q = src_q[rows_qs, :] * ATT_SCALE
        qs = jnp.concatenate([jnp.where(first, q, 0.0), jnp.where(first, 0.0, q)], axis=0).astype(BF16)
        kk = src_k[rows_k, :].astype(BF16)
        s = lax.dot_general(qs, kk, (((1,), (1,)), ((), ())), preferred_element_type=F32)
        yield
        s = s + bias
        m = jnp.max(s, axis=-1, keepdims=True)
        yield
        p = jnp.exp(s - m).astype(BF16)
        yield
        vv = src_v[rows_k, :].astype(BF16)
        pv = jnp.dot(p, jnp.concatenate([vv, ones[:n_keys]], axis=1), preferred_element_type=F32)
        yield
        mb = jnp.broadcast_to(m, (2 * QB, LANES))
        num_ref[br, rows_q, :] = jnp.where(first, pv[:QB, :LANES], pv[QB:, :LANES])
        den_ref[br, rows_q, :] = jnp.where(first, pv[:QB, LANES:], pv[QB:, LANES:])
        max_ref[br, rows_q, :] = jnp.where(first, mb[:QB], mb[QB:])

    units, last_unit_of_chunk = [], {}
    for br, (window, dil) in sorted(enumerate(DILATIONS), key=lambda e: -e[1][1]):
        assert window // dil == QB
        sub_len = seq // dil
        n_blocks = sub_len // QB
        assert not (_needs_pitch(dil) and n_blocks > 1)
        for r in range(dil):
            for n in range(n_blocks):
                q_start = r + dil * n * QB
                for row in range(q_start, q_start + dil * QB, dil):
                    last_unit_of_chunk[row // COMBINE_ROWS] = len(units)
                if n == 0:
                    units.append(unit(br, dil, q_start, r, QB, bias_diag))
                else:
                    units.append(unit(br, dil, q_start, q_start - dil * QB, 2 * QB, bias_band))

    def combine(i):
        rows = pl.ds(i * COMBINE_ROWS, COMBINE_ROWS)
        m0, m1, m2 = max_ref[0, rows, :], max_ref[1, rows, :], max_ref[2, rows, :]
        mx = jnp.maximum(jnp.maximum(m0, m1), m2)
        e0, e1, e2 = jnp.exp(m0 - mx), jnp.exp(m1 - mx), jnp.exp(m2 - mx)
        num = e0 * num_ref[0, rows, :] + e1 * num_ref[1, rows, :] + e2 * num_ref[2, rows, :]
        den = e0 * den_ref[0, rows, :] + e1 * den_ref[1, rows, :] + e2 * den_ref[2, rows, :]
        o_ref[rows, :] = (num / den).astype(o_ref.dtype)

    n_stages = 5
    pending = sorted(range(seq // COMBINE_ROWS), key=lambda i: last_unit_of_chunk[i])
    step = 0
    for _ in _skewed(units, n_stages):
        yield
        while pending and last_unit_of_chunk[pending[0]] + n_stages - 1 <= step:
            combine(pending.pop(0))
            yield
        step += 1
    assert not pending


ATT_STEPS_PER_RET_CHUNK = 1


def _mixer_kernel(qr_ref, kr_ref, vr_ref, gr_ref, qa_ref, ka_ref, va_ref, r_ref, st_ref, o_ref,
                  num_ref, den_ref, max_ref, qp_ref, kp_ref, vp_ref, *, seq):
    for _ in _round_robin([
            (_ret_body(qr_ref, kr_ref, vr_ref, gr_ref, r_ref, st_ref, pl.program_id(1), seq // RET_CHUNK), 1),
            (_att_body(qa_ref, ka_ref, va_ref, o_ref, num_ref, den_ref, max_ref, qp_ref, kp_ref, vp_ref, seq),
             ATT_STEPS_PER_RET_CHUNK)]):
        pass


def _mixer_call(qr, kr, vr, gr, qa, ka, va, batch, seq):
    assert H_RET == H_ATT and seq % RET_CHUNK == 0 and seq % COMBINE_ROWS == 0
    blk = pl.BlockSpec((seq, LANES), lambda b, p: (b, p))
    n_br = len(DILATIONS)
    pitched_rows = max([seq // dil * (dil + 1) for _, dil in DILATIONS if _needs_pitch(dil)] + [8])
    return pl.pallas_call(
        functools.partial(_mixer_kernel, seq=seq),
        grid=(batch, H_RET // HEADS_PER_TILE),
        in_specs=[blk] * 7,
        out_specs=[blk, pl.BlockSpec((1, HEADS_PER_TILE, DK_RET, DK_RET), lambda b, p: (b, p, 0, 0)), blk],
        out_shape=[jax.ShapeDtypeStruct((batch * seq, RET_WIDTH), BF16),
                   jax.ShapeDtypeStruct((batch, H_RET, DK_RET, DK_RET), F32),
                   jax.ShapeDtypeStruct((batch * seq, ATT_WIDTH), BF16)],
        scratch_shapes=[pltpu.VMEM((n_br, seq, LANES), F32)] * 3 + [pltpu.VMEM((pitched_rows, LANES), F32)] * 3,
        compiler_params=pltpu.CompilerParams(dimension_semantics=("parallel", "parallel"),
                                             vmem_limit_bytes=VMEM_LIMIT),
    )(qr, kr, vr, gr, qa, ka, va)


def _post_kernel(*refs, tm, tiles_per_seq, shift):
    (x_ref, r_ref, a_ref, p_ref, lng_ref, lnb_ref, wo_ref, g1_ref, b1_ref, wup_ref, cw_ref, cb_ref,
     wdn_ref, wpg_ref, wpp_ref, g2_ref, b2_ref) = refs[:17]
    if shift:
        y_ref, cs_ref, h1b_s, halo_s = refs[17:]
    else:
        pre0_ref, pre1_ref, y_ref, cs_ref, h1b_s = refs[17:]

    h = _layer_norm(x_ref[...], lng_ref[...], lnb_ref[...])
    mix = (jnp.dot(r_ref[...].astype(BF16), wo_ref[:RET_WIDTH, :], preferred_element_type=F32)
           + jnp.dot(a_ref[...].astype(BF16), wo_ref[RET_WIDTH:, :], preferred_element_type=F32))
    h1 = _layer_norm(ALPHA * h + mix, g1_ref[...], b1_ref[...])
    h1b_s[...] = h1.astype(BF16)
    gate = _sigmoid(jnp.dot(h1b_s[...], wpg_ref[...], preferred_element_type=F32))
    y_ref[...] = ALPHA * h1 + gate * jnp.dot(p_ref[...].astype(BF16), wpp_ref[...], preferred_element_type=F32)

    if shift:
        @pl.when(pl.program_id(0) % tiles_per_seq == 0)
        def _():
            halo_s[...] = jnp.zeros_like(halo_s)
        row8 = lax.broadcasted_iota(jnp.int32, (8, FF_CHUNK), 0)

    n_chunks = D_FF // FF_CHUNK
    partials, acts = [], []
    cols_of = lambda part, j: slice(part * D_FF + j * FF_CHUNK, part * D_FF + (j + 1) * FF_CHUNK)
    kw = D_MODEL // UP_K_SPLIT
    grp = min(tm, ACT_GROUP_ROWS)
    u_of = {}

    def up_gen(j):
        for part in (0, 1):
            acc = None
            for kq in range(UP_K_SPLIT):
                d = jnp.dot(h1b_s[:, kq * kw:(kq + 1) * kw], wup_ref[kq * kw:(kq + 1) * kw, cols_of(part, j)],
                            preferred_element_type=F32)
                acc = d if acc is None else acc + d
                yield
            u_of[j, part] = acc

    def act_gen(j, out):
        pieces = []
        for r0 in range(0, tm, grp):
            conv = []
            for part in (0, 1):
                cols = cols_of(part, j)
                u = u_of[j, part][r0:r0 + grp]
                if shift:
                    above = halo_s[:, cols] if r0 == 0 else u_of[j, part][r0 - 8:r0]

                    def shifted(k, u=u, above=above):
                        rolled = pltpu.roll(u, k, 0)
                        top = jnp.where(row8 < k, pltpu.roll(above, k, 0), rolled[0:8])
                        return jnp.concatenate([top, rolled[8:]], axis=0)

                    prev2, prev1 = shifted(2), shifted(1)
                    if r0 + grp == tm:
                        halo_s[:, cols] = u[grp - 8:, :]
                        cs_ref[0, :, cols] = u[grp - (CONV_W - 1):, :]
                else:
                    cs_ref[r0:r0 + grp, cols] = u
                    prev2 = pre0_ref[r0:r0 + grp, cols]
                    prev1 = pre1_ref[r0:r0 + grp, cols]
                conv.append(cb_ref[:, cols] + cw_ref[0:1, cols] * prev2 + cw_ref[1:2, cols] * prev1
                            + cw_ref[2:3, cols] * u)
                yield
            pieces.append((conv[0] * _gelu_tanh(conv[1])).astype(BF16))
            yield
        del u_of[j, 0], u_of[j, 1]
        out.append(jnp.concatenate(pieces, axis=0))

    for j in range(n_chunks + 1):
        for _ in _round_robin(([(up_gen(j), 1)] if j < n_chunks else [])
                              + ([(act_gen(j - 1, acts), ACT_STEPS_PER_UP_STEP)] if j else [])):
            pass
        if j and (len(acts) == DOWN_GROUP or j == n_chunks):
            r0 = (j - len(acts)) * FF_CHUNK
            partials.append(jnp.dot(jnp.concatenate(acts, axis=1), wdn_ref[r0:j * FF_CHUNK, :],
                                    preferred_element_type=F32))
            acts = []

    y_ref[...] = _layer_norm(y_ref[...] + sum(partials[1:], partials[0]), g2_ref[...], b2_ref[...])


def _post_call(x2d, r2d, a2d, p2d, weights, tm, tiles_per_seq, prefix=None):
    n = x2d.shape[0]
    shift = prefix is None
    two_f = 2 * D_FF
    row = lambda i: (i, 0)
    in_specs = [pl.BlockSpec((tm, D_MODEL), row), pl.BlockSpec((tm, RET_WIDTH), row),
                pl.BlockSpec((tm, ATT_WIDTH), row), pl.BlockSpec((tm, PLE_DIM), row)]
    in_specs += [_const_spec(w.shape) for w in weights]
    args = [x2d, r2d, a2d, p2d, *weights]
    scratch = [pltpu.VMEM((tm, D_MODEL), BF16)]
    if shift:
        out_specs = [pl.BlockSpec((tm, D_MODEL), row),
                     pl.BlockSpec((1, CONV_W - 1, two_f), lambda i: (i // tiles_per_seq, 0, 0))]
        out_shape = [jax.ShapeDtypeStruct((n, D_MODEL), F32),
                     jax.ShapeDtypeStruct((n // (tm * tiles_per_seq), CONV_W - 1, two_f), F32)]
        scratch += [pltpu.VMEM((8, two_f), F32)]
    else:
        in_specs += [pl.BlockSpec((tm, two_f), row)] * 2
        args += list(prefix)
        out_specs = [pl.BlockSpec((tm, D_MODEL), row), pl.BlockSpec((tm, two_f), row)]
        out_shape = [jax.ShapeDtypeStruct((n, D_MODEL), F32), jax.ShapeDtypeStruct((n, two_f), F32)]
    return pl.pallas_call(
        functools.partial(_post_kernel, tm=tm, tiles_per_seq=tiles_per_seq, shift=shift),
        grid=(n // tm,),
        in_specs=in_specs,
        out_specs=out_specs,
        out_shape=out_shape,
        scratch_shapes=scratch,
        compiler_params=pltpu.CompilerParams(dimension_semantics=("arbitrary",), vmem_limit_bytes=VMEM_LIMIT),
    )(*args)


def _sample_mix_kernel(qr_ref, kr_ref, vr_ref, gr_ref, st_ref, qa_ref, ka_ref, qat_ref, vat_ref,
                       kt_ref, vt_ref, r_ref, att_ref, nst_ref, s_s, pc_s):
    hrow = lax.broadcasted_iota(jnp.int32, (H_RET, RET_WIDTH), 0)
    hlane = lax.broadcasted_iota(jnp.int32, (H_RET, RET_WIDTH), 1) // DK_RET
    own = hrow == hlane

    qm = jnp.where(own, qr_ref[0], 0.0)
    km = jnp.where(own, kr_ref[0], 0.0)
    v8 = vr_ref[0]
    g8 = gr_ref[0]
    st = st_ref[0]
    lg8 = _select_log_gamma(lax.broadcasted_iota(jnp.int32, (H_RET, 1), 0))
    cross = jnp.dot(qm.astype(BF16), st.astype(BF16), preferred_element_type=F32) * jnp.exp(lg8)
    qk = jnp.sum(qm * km, axis=-1, keepdims=True)
    o = qk * v8 + cross
    lg_rows = _select_log_gamma(lax.broadcasted_iota(jnp.int32, (H_RET * DK_RET, 1), 0) // DK_RET)
    outer = lax.dot_general(km, v8, (((0,), (0,)), ((), ())), preferred_element_type=F32,
                            precision=lax.Precision.HIGHEST)
    nst_ref[0] = jnp.exp(lg_rows) * st + outer
    mu = jnp.mean(o, axis=-1, keepdims=True)
    xc = o - mu
    var = jnp.mean(xc * xc, axis=-1, keepdims=True)
    r_ref[0] = xc * lax.rsqrt(var + LN_EPS) * (g8 * _sigmoid(g8))

    n_past = kt_ref.shape[-1]
    s_new = jnp.sum(qa_ref[0] * ka_ref[0], axis=-1, keepdims=True) * ATT_SCALE
    qt = qat_ref[0] * ATT_SCALE
    for h in range(H_ATT):
        s_s[h:h + 1, :] = jnp.sum(kt_ref[h] * qt[:, h:h + 1], axis=0, keepdims=True)
    tok = lax.broadcasted_iota(jnp.int32, (H_ATT, n_past), 1)
    stats = []
    for window, dil in DILATIONS:
        lo = n_past - window
        sb = s_s[:, lo:] + jnp.where(tok[:, lo:] % dil == n_past % dil, 0.0, NEG_BIG)
        m = jnp.maximum(jnp.max(sb, axis=-1, keepdims=True), s_new)
        p = jnp.exp(sb - m)
        p_new = jnp.exp(s_new - m)
        den = jnp.sum(p, axis=-1, keepdims=True) + p_new
        stats.append((lo, p, p_new, den, m + jnp.log(den)))
    mx = jnp.maximum(jnp.maximum(stats[0][4], stats[1][4]), stats[2][4])
    es = [jnp.exp(st_[4] - mx) for st_ in stats]
    tot = es[0] + es[1] + es[2]
    coef = [e / (tot * st_[3]) for e, st_ in zip(es, stats)]
    w_new = coef[0] * stats[0][2] + coef[1] * stats[1][2] + coef[2] * stats[2][2]
    order = sorted(range(len(stats)), key=lambda i: stats[i][0])
    assert stats[order[0]][0] == 0
    pc_s[...] = coef[order[0]] * stats[order[0]][1]
    for i in order[1:]:
        lo = stats[i][0]
        pc_s[:, lo:] = pc_s[:, lo:] + coef[i] * stats[i][1]
    vt_new = vat_ref[0]
    for h in range(H_ATT):
        col = jnp.sum(vt_ref[h] * pc_s[h:h + 1, :], axis=-1, keepdims=True)
        att_ref[0, :, h:h + 1] = col + w_new[h:h + 1, :] * vt_new[:, h:h + 1]


def _sample_mix_call(qr, kr, vr, gr, state, qa, ka, va, cache_k, cache_v):
    nb = qr.shape[0]
    n_past = cache_k.shape[-1]
    assert all(w <= n_past and n_past % d == 0 for w, d in DILATIONS) and max(w for w, _ in DILATIONS) == n_past
    row3 = pl.BlockSpec((1, 1, RET_WIDTH), lambda b: (b, 0, 0))
    head3 = pl.BlockSpec((1, H_RET, DK_RET), lambda b: (b, 0, 0))
    col3 = pl.BlockSpec((1, HD_ATT, H_ATT), lambda b: (b, 0, 0))
    st_spec = pl.BlockSpec((1, H_RET * DK_RET, DK_RET), lambda b: (b, 0, 0))
    cache_spec = pl.BlockSpec((None, H_ATT, HD_ATT, n_past), lambda b: (b, 0, 0, 0))
    r3 = lambda a: a.reshape(nb, 1, RET_WIDTH)
    h3 = lambda a: a.reshape(nb, H_RET, DK_RET)
    t3 = lambda a: jnp.swapaxes(h3(a), 1, 2)
    r_s, att_t, nst = pl.pallas_call(
        _sample_mix_kernel,
        grid=(nb,),
        in_specs=[row3, row3, head3, head3, st_spec, head3, head3, col3, col3, cache_spec, cache_spec],
        out_specs=[head3, col3, st_spec],
        out_shape=[jax.ShapeDtypeStruct((nb, H_RET, DK_RET), F32),
                   jax.ShapeDtypeStruct((nb, HD_ATT, H_ATT), F32),
                   jax.ShapeDtypeStruct((nb, H_RET * DK_RET, DK_RET), F32)],
        scratch_shapes=[pltpu.VMEM((H_ATT, n_past), F32), pltpu.VMEM((H_ATT, n_past), F32)],
        compiler_params=pltpu.CompilerParams(dimension_semantics=("parallel",), vmem_limit_bytes=VMEM_LIMIT),
    )(r3(qr), r3(kr), h3(vr), h3(gr), state.reshape(nb, H_RET * DK_RET, DK_RET), h3(qa), h3(ka), t3(qa), t3(va),
      cache_k, cache_v)
    return r_s, jnp.swapaxes(att_t, 1, 2), nst


def kernel(x_prompt, x_sample, cache_k_win, cache_v_win, state_ret, state_conv, p_prompt, p_sample,
           ln_in_g, ln_in_b, w_in, w_out, ln1_g, ln1_b, w_up, conv_w, conv_b, w_down,
           w_ple_gate, w_ple_proj, ln2_g, ln2_b):
    B, S, _ = x_prompt.shape
    NB, T, _ = x_sample.shape
    assert T == 1 and w_in.shape[0] == DEPTH == 1 and S % PROJ_TILE == 0 and S % POST_TILE == 0
    two_f = 2 * D_FF
    vec = lambda a: a.reshape(1, -1)
    w_in_b = w_in[0].astype(BF16)
    post_w = (vec(ln_in_g), vec(ln_in_b), w_out[0].astype(BF16), vec(ln1_g[0]), vec(ln1_b[0]),
              w_up[0].astype(BF16), conv_w[0], vec(conv_b[0]), w_down[0].astype(BF16),
              w_ple_gate[0].astype(BF16), w_ple_proj[0].astype(BF16), vec(ln2_g[0]), vec(ln2_b[0]))

    pos_p = jnp.arange(S, dtype=jnp.int32)
    tabs_p = _rot_tables(pos_p, DK_RET, RET_ROPE_BASE) + _rot_tables(pos_p, ROT_DIMS, ATT_ROPE_THETA)
    xp = x_prompt.reshape(B * S, D_MODEL)
    qr, kr, vr, gr, qa, ka, va, ka_t, va_t = _proj_call(
        xp, vec(ln_in_g), vec(ln_in_b), w_in_b, tabs_p, PROJ_TILE, S // PROJ_TILE,
        (BF16, F32, BF16, F32, F32, F32, F32), seq_for_transposed=S)
    r_p, ret_fin, att_p = _mixer_call(qr, kr, vr, gr, qa, ka, va, B, S)
    y_p, conv_p = _post_call(xp, r_p, att_p, p_prompt[0].reshape(B * S, PLE_DIM), post_w,
                             POST_TILE, S // POST_TILE)
    keep = min(WINDOW_MAX, S)
    k_win_p = jnp.transpose(ka_t, (0, 3, 1, 2))[:, S - keep:]
    v_win_p = jnp.transpose(va_t, (0, 3, 1, 2))[:, S - keep:]

    pos_s = jnp.full((NB,), PAST_LEN, jnp.int32)
    tabs_s = _rot_tables(pos_s, DK_RET, RET_ROPE_BASE) + _rot_tables(pos_s, ROT_DIMS, ATT_ROPE_THETA)
    xs = x_sample.reshape(NB, D_MODEL)
    sqr, skr, svr, sgr, sqa, ska, sva = _proj_call(xs, vec(ln_in_g), vec(ln_in_b), w_in_b, tabs_s, NB, 1,
                                                   (F32,) * 7)
    r_s, att_s, nst = _sample_mix_call(sqr, skr, svr, sgr, state_ret[0], sqa, ska, sva,
                                       jnp.transpose(cache_k_win[0], (0, 2, 3, 1)),
                                       jnp.transpose(cache_v_win[0], (0, 2, 3, 1)))
    y_s, u_s = _post_call(xs, r_s.reshape(NB, RET_WIDTH), att_s.reshape(NB, ATT_WIDTH), p_sample[0].reshape(NB, PLE_DIM),
                          post_w, NB, 1, prefix=(state_conv[0][:, 0], state_conv[0][:, 1]))
    conv_s = jnp.stack([state_conv[0][:, 1], u_s], axis=1)

    return (y_p.reshape(B, S, D_MODEL), y_s.reshape(NB, 1, D_MODEL),
            k_win_p[None], v_win_p[None], ret_fin[None], conv_p[None],
            ska.reshape(1, NB, 1, H_ATT, HD_ATT), sva.reshape(1, NB, 1, H_ATT, HD_ATT),
            nst.reshape(1, NB, H_RET, DK_RET, DK_RET), conv_s[None])
```

```python
import functools
import math

import numpy as np
import jax
import jax.numpy as jnp
from jax import lax
from jax.experimental import pallas as pl
from jax.experimental.pallas import tpu as pltpu

F32 = jnp.float32
BF16 = jnp.bfloat16

D_MODEL = 1024
PAST_LEN = 16384
H_RET = 8
DK_RET = 64
RET_WIDTH = 512
RET_CHUNK = 128
RET_ROPE_BASE = 10000.0
H_ATT = 8
HD_ATT = 64
ATT_WIDTH = 512
ATT_ROPE_THETA = 500000.0
ROT_DIMS = HD_ATT // 4
DILATIONS = ((128, 1), (512, 4), (2048, 16))
WINDOW_MAX = 2048
Q_BLOCK = 128
PROJ_WIDTH = 4 * RET_WIDTH + 3 * ATT_WIDTH
D_FF = 2816
CONV_W = 3
PLE_DIM = 256
LN_EPS = 1e-5
DEPTH = 1
ALPHA = (2 * DEPTH) ** 0.25

LANES = 128
HEADS_PER_TILE = LANES // DK_RET
NEG_BIG = -1e30
LOG_GAMMA = tuple(math.log(1.0 - 2.0 ** (-5.0 - h)) for h in range(H_RET))
ATT_SCALE = HD_ATT ** -0.5
VMEM_LIMIT = 56 * 1024 * 1024

PROJ_TILE = 512
POST_TILE = 512
FF_CHUNK = 256
DOWN_GROUP = 6
UP_K_SPLIT = 2
ACT_GROUP_ROWS = 128
ACT_STEPS_PER_UP_STEP = 3


def _layer_norm(x, g, b):
    mu = jnp.mean(x, axis=-1, keepdims=True)
    xc = x - mu
    var = jnp.mean(xc * xc, axis=-1, keepdims=True)
    return xc * lax.rsqrt(var + LN_EPS) * g + b


def _sigmoid(x):
    return 1.0 / (1.0 + jnp.exp(-x))


def _gelu_tanh(x):
    c0 = math.sqrt(2.0 / math.pi)
    return x * (0.5 + 0.5 * jnp.tanh(x * (c0 + (c0 * 0.044715) * (x * x))))


def _select_log_gamma(head_idx):
    out = jnp.zeros(head_idx.shape, F32)
    for h in range(H_RET):
        out = jnp.where(head_idx == h, LOG_GAMMA[h], out)
    return out


def _const_spec(shape):
    nd = len(shape)
    return pl.BlockSpec(shape, lambda *_: (0,) * nd, pipeline_mode=pl.Buffered(1))


def _rotate(z, cos, s_lo, s_hi, half):
    up = pltpu.roll(z, LANES - half, 1)
    dn = pltpu.roll(z, half, 1)
    return z * cos + up * s_lo + dn * s_hi


PROJ_GROUP_ORDER = (5, 6, 0, 1, 4, 3, 2)


def _proj_kernel(x_ref, g_ref, b_ref, w_ref, cr_ref, slr_ref, shr_ref, ca_ref, sla_ref, sha_ref,
                 qr_ref, kr_ref, vr_ref, gr_ref, qa_ref, ka_ref, va_ref, *t_refs):
    outs = (qr_ref, kr_ref, vr_ref, gr_ref, qa_ref, ka_ref, va_ref)
    tm = x_ref.shape[0]
    cols = lambda grp: slice(grp * RET_WIDTH, (grp + 1) * RET_WIDTH)
    part = tm // 2 if tm % 32 == 0 else tm
    hb_parts = [_layer_norm(x_ref[r0:r0 + part], g_ref[...], b_ref[...]).astype(BF16) for r0 in range(0, tm, part)]
    z_next = jnp.concatenate([jnp.dot(hp, w_ref[:, cols(PROJ_GROUP_ORDER[0])], preferred_element_type=F32)
                              for hp in hb_parts], axis=0)
    hb = jnp.concatenate(hb_parts, axis=0)
    for i, grp in enumerate(PROJ_GROUP_ORDER):
        o_ref = outs[grp]
        zg = z_next
        if i + 1 < len(PROJ_GROUP_ORDER):
            z_next = jnp.dot(hb, w_ref[:, cols(PROJ_GROUP_ORDER[i + 1])], preferred_element_type=F32)
        for j in range(RET_WIDTH // LANES):
            z = zg[:, j * LANES:(j + 1) * LANES]
            if grp in (0, 1):
                z = _rotate(z, cr_ref[...], slr_ref[...], shr_ref[...], DK_RET // 2)
                if grp == 1:
                    z = z * (DK_RET ** -0.5)
            elif grp in (4, 5):
                z = _rotate(z, ca_ref[...], sla_ref[...], sha_ref[...], ROT_DIMS // 2)
            o_ref[:, j * LANES:(j + 1) * LANES] = z.astype(o_ref.dtype)
            if t_refs and grp in (5, 6):
                zt = z.T
                for hh in range(HEADS_PER_TILE):
                    t_refs[grp - 5][0, HEADS_PER_TILE * j + hh] = zt[hh * HD_ATT:(hh + 1) * HD_ATT]


def _rot_tables(pos, n_rot, base):
    half = n_rot // 2
    inv = base ** (-jnp.arange(half, dtype=F32) / half)
    ang = pos.astype(F32)[:, None] * inv[None, :]
    cos, sin = jnp.cos(ang), jnp.sin(ang)
    l = np.arange(LANES) % DK_RET
    idx = np.where(l < half, l, np.where(l < n_rot, l - half, 0))
    lo = jnp.asarray(l < half)
    hi = jnp.asarray((l >= half) & (l < n_rot))
    cos_f = jnp.where(jnp.asarray(l < n_rot), cos[:, idx], 1.0)
    s_lo = jnp.where(lo, -sin[:, idx], 0.0)
    s_hi = jnp.where(hi, sin[:, idx], 0.0)
    return cos_f, s_lo, s_hi


def _proj_call(x2d, ln_g, ln_b, w_in_b, tabs, tm, tab_blocks, out_dtypes, seq_for_transposed=None):
    n = x2d.shape[0]
    row = lambda i: (i, 0)
    tab = lambda i: (i % tab_blocks, 0)
    tab_spec = pl.BlockSpec((tm, LANES), tab)
    out_specs = [pl.BlockSpec((tm, RET_WIDTH), row)] * 7
    out_shape = [jax.ShapeDtypeStruct((n, RET_WIDTH), dt) for dt in out_dtypes]
    if seq_for_transposed is not None:
        tps = seq_for_transposed // tm
        out_specs += [pl.BlockSpec((1, H_ATT, HD_ATT, tm), lambda i: (i // tps, 0, 0, i % tps))] * 2
        out_shape += [jax.ShapeDtypeStruct((n // seq_for_transposed, H_ATT, HD_ATT, seq_for_transposed), F32)] * 2
    return pl.pallas_call(
        _proj_kernel,
        grid=(n // tm,),
        in_specs=[pl.BlockSpec((tm, D_MODEL), row), _const_spec((1, D_MODEL)), _const_spec((1, D_MODEL)),
                  _const_spec((D_MODEL, PROJ_WIDTH))] + [tab_spec] * 6,
        out_specs=out_specs,
        out_shape=out_shape,
        compiler_params=pltpu.CompilerParams(dimension_semantics=("parallel",), vmem_limit_bytes=VMEM_LIMIT),
    )(x2d, ln_g, ln_b, w_in_b, *tabs)


COMBINE_ROWS = 256


def _round_robin(entries):
    live = list(entries)
    while live:
        for entry in list(live):
            for _ in range(entry[1]):
                if next(entry[0], live) is live:
                    live.remove(entry)
                    break
                yield


def _skewed(items, n_stages):
    for t in range(len(items) + n_stages - 1):
        for stage in range(n_stages):
            if 0 <= t - stage < len(items):
                next(items[t - stage], None)
        yield


def _ret_body(q_ref, k_ref, v_ref, g_ref, r_ref, st_ref, hp, n_chunks):
    C = RET_CHUNK
    lane1 = lax.broadcasted_iota(jnp.int32, (1, LANES), 1)
    lg_lane = _select_log_gamma(HEADS_PER_TILE * hp + (lane1 >= DK_RET).astype(jnp.int32))

    ri = lax.broadcasted_iota(jnp.int32, (2 * C, C), 0)
    ci = lax.broadcasted_iota(jnp.int32, (2 * C, C), 1)
    rel = (jnp.where(ri >= C, ri - C, ri) - ci).astype(F32)
    lg_rows = _select_log_gamma(HEADS_PER_TILE * hp + (ri >= C).astype(jnp.int32))
    decay = jnp.where(rel >= 0, jnp.exp(lg_rows * jnp.maximum(rel, 0.0)), 0.0)

    tok = lax.broadcasted_iota(jnp.int32, (C, LANES), 0).astype(F32)
    cross_dec = jnp.exp(lg_lane * (tok + 1.0))
    k_dec = jnp.exp(lg_lane * (C - 1.0 - tok))
    sr = lax.broadcasted_iota(jnp.int32, (LANES, LANES), 0)
    sc = lax.broadcasted_iota(jnp.int32, (LANES, LANES), 1)
    same_head = ((sr >= DK_RET) == (sc >= DK_RET)).astype(F32)
    state_dec = jnp.exp(_select_log_gamma(HEADS_PER_TILE * hp + (sr >= DK_RET).astype(jnp.int32)) * float(C))

    states = [jnp.zeros((LANES, LANES), F32)]

    def chunk(c):
        first = lax.broadcasted_iota(jnp.int32, (1, LANES), 1) < DK_RET
        off = c * C
        q = q_ref[pl.ds(off, C), :]
        k = k_ref[pl.ds(off, C), :]
        v = v_ref[pl.ds(off, C), :]
        zero = jnp.zeros_like(q)
        qs = jnp.concatenate([jnp.where(first, q, zero), jnp.where(first, zero, q)], axis=0)
        s = lax.dot_general(qs, k.astype(BF16), (((1,), (1,)), ((), ())), preferred_element_type=F32)
        upd = lax.dot_general((k * k_dec).astype(BF16), v, (((0,), (0,)), ((), ())),
                              preferred_element_type=F32)
        states.append(state_dec * states[c] + same_head * upd)
        yield
        pv = jnp.dot((s * decay).astype(BF16), v, preferred_element_type=F32)
        cross = jnp.dot(q, states[c].astype(BF16), preferred_element_type=F32) * cross_dec
        yield
        o = jnp.where(first, pv[:C], pv[C:]) + cross
        yield
        s_a = jnp.sum(jnp.where(first, o, 0.0), axis=-1, keepdims=True)
        s_b = jnp.sum(jnp.where(first, 0.0, o), axis=-1, keepdims=True)
        xc = o - jnp.where(first, s_a, s_b) * (1.0 / DK_RET)
        sq = xc * xc
        yield
        v_a = jnp.sum(jnp.where(first, sq, 0.0), axis=-1, keepdims=True)
        v_b = jnp.sum(jnp.where(first, 0.0, sq), axis=-1, keepdims=True)
        rn = xc * lax.rsqrt(jnp.where(first, v_a, v_b) * (1.0 / DK_RET) + LN_EPS)
        g = g_ref[pl.ds(off, C), :]
        r_ref[pl.ds(off, C), :] = (rn * (g * _sigmoid(g))).astype(r_ref.dtype)

    yield from _skewed([chunk(c) for c in range(n_chunks)], 5)
    st_ref[0, 0] = states[n_chunks][:DK_RET, :DK_RET]
    st_ref[0, 1] = states[n_chunks][DK_RET:, DK_RET:]


def _needs_pitch(dil):
    return dil % 8 == 0


def _att_body(q_ref, k_ref, v_ref, o_ref, num_ref, den_ref, max_ref, qp_ref, kp_ref, vp_ref, seq):
    QB = Q_BLOCK
    pitched = [dil for _, dil in DILATIONS if _needs_pitch(dil)]
    assert len(pitched) <= 1
    for dil in pitched:
        for g in range(seq // dil):
            for src, dst in ((q_ref, qp_ref), (k_ref, kp_ref), (v_ref, vp_ref)):
                dst[g * (dil + 1):g * (dil + 1) + dil, :] = src[g * dil:(g + 1) * dil, :]
    ri = lax.broadcasted_iota(jnp.int32, (2 * QB, 2 * QB), 0)
    ci = lax.broadcasted_iota(jnp.int32, (2 * QB, 2 * QB), 1)
    qi = jnp.where(ri >= QB, ri - QB, ri)
    valid_band = ((ci < QB) & (ci >= qi)) | ((ci >= QB) & (ci - QB <= qi))
    bias_band = jnp.where(valid_band, 0.0, NEG_BIG)
    rd = lax.broadcasted_iota(jnp.int32, (2 * QB, QB), 0)
    cd = lax.broadcasted_iota(jnp.int32, (2 * QB, QB), 1)
    bias_diag = jnp.where(cd <= jnp.where(rd >= QB, rd - QB, rd), 0.0, NEG_BIG)
    ones = jnp.ones((2 * QB, LANES), BF16)

    def unit(br, dil, q_start, k_start, n_keys, bias):
        lane = lax.broadcasted_iota(jnp.int32, (1, LANES), 1)
        first = lane < HD_ATT
        rows_q = pl.ds(q_start, QB, stride=dil) if dil > 1 else pl.ds(q_start, QB)
        if _needs_pitch(dil):
            src_q, src_k, src_v = qp_ref, kp_ref, vp_ref
            rows_qs = pl.ds(q_start, QB, stride=dil + 1)
            rows_k = pl.ds(k_start, n_keys, stride=dil + 1)
        else:
            src_q, src_k, src_v = q_ref, k_ref, v_ref
            rows_qs = rows_q
            rows_k = pl.ds(k_start, n_keys, stride=dil) if dil > 1 else pl.ds(k_start, n_keys)
        q = src_q[rows_qs, :] * ATT_SCALE
        qs = jnp.concatenate([jnp.where(first, q, 0.0), jnp.where(first, 0.0, q)], axis=0).astype(BF16)
        kk = src_k[rows_k, :].astype(BF16)
        s = lax.dot_general(qs, kk, (((1,), (1,)), ((), ())), preferred_element_type=F32)
        yield
        s = s + bias
        m = jnp.max(s, axis=-1, keepdims=True)
        yield
        p = jnp.exp(s - m).astype(BF16)
        yield
        vv = src_v[rows_k, :].astype(BF16)
        pv = jnp.dot(p, jnp.concatenate([vv, ones[:n_keys]], axis=1), preferred_element_type=F32)
        yield
        mb = jnp.broadcast_to(m, (2 * QB, LANES))
        num_ref[br, rows_q, :] = jnp.where(first, pv[:QB, :LANES], pv[QB:, :LANES])
        den_ref[br, rows_q, :] = jnp.where(first, pv[:QB, LANES:], pv[QB:, LANES:])
        max_ref[br, rows_q, :] = jnp.where(first, mb[:QB], mb[QB:])

    units, last_unit_of_chunk = [], {}
    for br, (window, dil) in sorted(enumerate(DILATIONS), key=lambda e: -e[1][1]):
        assert window // dil == QB
        sub_len = seq // dil
        n_blocks = sub_len // QB
        assert not (_needs_pitch(dil) and n_blocks > 1)
        for r in range(dil):
            for n in range(n_blocks):
                q_start = r + dil * n * QB
                for row in range(q_start, q_start + dil * QB, dil):
                    last_unit_of_chunk[row // COMBINE_ROWS] = len(units)
                if n == 0:
                    units.append(unit(br, dil, q_start, r, QB, bias_diag))
                else:
                    units.append(unit(br, dil, q_start, q_start - dil * QB, 2 * QB, bias_band))

    def combine(i):
        rows = pl.ds(i * COMBINE_ROWS, COMBINE_ROWS)
        m0, m1, m2 = max_ref[0, rows, :], max_ref[1, rows, :], max_ref[2, rows, :]
        mx = jnp.maximum(jnp.maximum(m0, m1), m2)
        e0, e1, e2 = jnp.exp(m0 - mx), jnp.exp(m1 - mx), jnp.exp(m2 - mx)
        num = e0 * num_ref[0, rows, :] + e1 * num_ref[1, rows, :] + e2 * num_ref[2, rows, :]
        den = e0 * den_ref[0, rows, :] + e1 * den_ref[1, rows, :] + e2 * den_ref[2, rows, :]
        o_ref[rows, :] = (num / den).astype(o_ref.dtype)

    n_stages = 5
    pending = sorted(range(seq // COMBINE_ROWS), key=lambda i: last_unit_of_chunk[i])
    step = 0
    for _ in _skewed(units, n_stages):
        yield
        while pending and last_unit_of_chunk[pending[0]] + n_stages - 1 <= step:
            combine(pending.pop(0))
            yield
        step += 1
    assert not pending


ATT_STEPS_PER_RET_CHUNK = 1


def _mixer_kernel(qr_ref, kr_ref, vr_ref, gr_ref, qa_ref, ka_ref, va_ref, r_ref, st_ref, o_ref,
                  num_ref, den_ref, max_ref, qp_ref, kp_ref, vp_ref, *, seq):
    for _ in _round_robin([
            (_ret_body(qr_ref, kr_ref, vr_ref, gr_ref, r_ref, st_ref, pl.program_id(1), seq // RET_CHUNK), 1),
            (_att_body(qa_ref, ka_ref, va_ref, o_ref, num_ref, den_ref, max_ref, qp_ref, kp_ref, vp_ref, seq),
             ATT_STEPS_PER_RET_CHUNK)]):
        pass


def _mixer_call(qr, kr, vr, gr, qa, ka, va, batch, seq):
    assert H_RET == H_ATT and seq % RET_CHUNK == 0 and seq % COMBINE_ROWS == 0
    blk = pl.BlockSpec((seq, LANES), lambda b, p: (b, p))
    n_br = len(DILATIONS)
    pitched_rows = max([seq // dil * (dil + 1) for _, dil in DILATIONS if _needs_pitch(dil)] + [8])
    return pl.pallas_call(
        functools.partial(_mixer_kernel, seq=seq),
        grid=(batch, H_RET // HEADS_PER_TILE),
        in_specs=[blk] * 7,
        out_specs=[blk, pl.BlockSpec((1, HEADS_PER_TILE, DK_RET, DK_RET), lambda b, p: (b, p, 0, 0)), blk],
        out_shape=[jax.ShapeDtypeStruct((batch * seq, RET_WIDTH), BF16),
                   jax.ShapeDtypeStruct((batch, H_RET, DK_RET, DK_RET), F32),
                   jax.ShapeDtypeStruct((batch * seq, ATT_WIDTH), BF16)],
        scratch_shapes=[pltpu.VMEM((n_br, seq, LANES), F32)] * 3 + [pltpu.VMEM((pitched_rows, LANES), F32)] * 3,
        compiler_params=pltpu.CompilerParams(dimension_semantics=("parallel", "parallel"),
                                             vmem_limit_bytes=VMEM_LIMIT),
    )(qr, kr, vr, gr, qa, ka, va)


def _post_kernel(*refs, tm, tiles_per_seq, shift):
    (x_ref, r_ref, a_ref, p_ref, lng_ref, lnb_ref, wo_ref, g1_ref, b1_ref, wup_ref, cw_ref, cb_ref,
     wdn_ref, wpg_ref, wpp_ref, g2_ref, b2_ref) = refs[:17]
    if shift:
        y_ref, cs_ref, h1b_s, halo_s = refs[17:]
    else:
        pre0_ref, pre1_ref, y_ref, cs_ref, h1b_s = refs[17:]

    h = _layer_norm(x_ref[...], lng_ref[...], lnb_ref[...])
    mix = (jnp.dot(r_ref[...].astype(BF16), wo_ref[:RET_WIDTH, :], preferred_element_type=F32)
           + jnp.dot(a_ref[...].astype(BF16), wo_ref[RET_WIDTH:, :], preferred_element_type=F32))
    h1 = _layer_norm(ALPHA * h + mix, g1_ref[...], b1_ref[...])
    h1b_s[...] = h1.astype(BF16)
    gate = _sigmoid(jnp.dot(h1b_s[...], wpg_ref[...], preferred_element_type=F32))
    y_ref[...] = ALPHA * h1 + gate * jnp.dot(p_ref[...].astype(BF16), wpp_ref[...], preferred_element_type=F32)

    if shift:
        @pl.when(pl.program_id(0) % tiles_per_seq == 0)
        def _():
            halo_s[...] = jnp.zeros_like(halo_s)
        row8 = lax.broadcasted_iota(jnp.int32, (8, FF_CHUNK), 0)

    n_chunks = D_FF // FF_CHUNK
    partials, acts = [], []
    cols_of = lambda part, j: slice(part * D_FF + j * FF_CHUNK, part * D_FF + (j + 1) * FF_CHUNK)
    kw = D_MODEL // UP_K_SPLIT
    grp = min(tm, ACT_GROUP_ROWS)
    u_of = {}

    def up_gen(j):
        for part in (0, 1):
            acc = None
            for kq in range(UP_K_SPLIT):
                d = jnp.dot(h1b_s[:, kq * kw:(kq + 1) * kw], wup_ref[kq * kw:(kq + 1) * kw, cols_of(part, j)],
                            preferred_element_type=F32)
                acc = d if acc is None else acc + d
                yield
            u_of[j, part] = acc

    def act_gen(j, out):
        pieces = []
        for r0 in range(0, tm, grp):
            conv = []
            for part in (0, 1):
                cols = cols_of(part, j)
                u = u_of[j, part][r0:r0 + grp]
                if shift:
                    above = halo_s[:, cols] if r0 == 0 else u_of[j, part][r0 - 8:r0]

                    def shifted(k, u=u, above=above):
                        rolled = pltpu.roll(u, k, 0)
                        top = jnp.where(row8 < k, pltpu.roll(above, k, 0), rolled[0:8])
                        return jnp.concatenate([top, rolled[8:]], axis=0)

                    prev2, prev1 = shifted(2), shifted(1)
                    if r0 + grp == tm:
                        halo_s[:, cols] = u[grp - 8:, :]
                        cs_ref[0, :, cols] = u[grp - (CONV_W - 1):, :]
                else:
                    cs_ref[r0:r0 + grp, cols] = u
                    prev2 = pre0_ref[r0:r0 + grp, cols]
                    prev1 = pre1_ref[r0:r0 + grp, cols]
                conv.append(cb_ref[:, cols] + cw_ref[0:1, cols] * prev2 + cw_ref[1:2, cols] * prev1
                            + cw_ref[2:3, cols] * u)
                yield
            pieces.append((conv[0] * _gelu_tanh(conv[1])).astype(BF16))
            yield
        del u_of[j, 0], u_of[j, 1]
        out.append(jnp.concatenate(pieces, axis=0))

    for j in range(n_chunks + 1):
        for _ in _round_robin(([(up_gen(j), 1)] if j < n_chunks else [])
                              + ([(act_gen(j - 1, acts), ACT_STEPS_PER_UP_STEP)] if j else [])):
            pass
        if j and (len(acts) == DOWN_GROUP or j == n_chunks):
            r0 = (j - len(acts)) * FF_CHUNK
            partials.append(jnp.dot(jnp.concatenate(acts, axis=1), wdn_ref[r0:j * FF_CHUNK, :],
                                    preferred_element_type=F32))
            acts = []

    y_ref[...] = _layer_norm(y_ref[...] + sum(partials[1:], partials[0]), g2_ref[...], b2_ref[...])


def _post_call(x2d, r2d, a2d, p2d, weights, tm, tiles_per_seq, prefix=None):
    n = x2d.shape[0]
    shift = prefix is None
    two_f = 2 * D_FF
    row = lambda i: (i, 0)
    in_specs = [pl.BlockSpec((tm, D_MODEL), row), pl.BlockSpec((tm, RET_WIDTH), row),
                pl.BlockSpec((tm, ATT_WIDTH), row), pl.BlockSpec((tm, PLE_DIM), row)]
    in_specs += [_const_spec(w.shape) for w in weights]
    args = [x2d, r2d, a2d, p2d, *weights]
    scratch = [pltpu.VMEM((tm, D_MODEL), BF16)]
    if shift:
        out_specs = [pl.BlockSpec((tm, D_MODEL), row),
                     pl.BlockSpec((1, CONV_W - 1, two_f), lambda i: (i // tiles_per_seq, 0, 0))]
        out_shape = [jax.ShapeDtypeStruct((n, D_MODEL), F32),
                     jax.ShapeDtypeStruct((n // (tm * tiles_per_seq), CONV_W - 1, two_f), F32)]
        scratch += [pltpu.VMEM((8, two_f), F32)]
    else:
        in_specs += [pl.BlockSpec((tm, two_f), row)] * 2
        args += list(prefix)
        out_specs = [pl.BlockSpec((tm, D_MODEL), row), pl.BlockSpec((tm, two_f), row)]
        out_shape = [jax.ShapeDtypeStruct((n, D_MODEL), F32), jax.ShapeDtypeStruct((n, two_f), F32)]
    return pl.pallas_call(
        functools.partial(_post_kernel, tm=tm, tiles_per_seq=tiles_per_seq, shift=shift),
        grid=(n // tm,),
        in_specs=in_specs,
        out_specs=out_specs,
        out_shape=out_shape,
        scratch_shapes=scratch,
        compiler_params=pltpu.CompilerParams(dimension_semantics=("arbitrary",), vmem_limit_bytes=VMEM_LIMIT),
    )(*args)


def _sample_mix_kernel(qr_ref, kr_ref, vr_ref, gr_ref, st_ref, qa_ref, ka_ref, qat_ref, vat_ref,
                       kt_ref, vt_ref, r_ref, att_ref, nst_ref, s_s, pc_s):
    hrow = lax.broadcasted_iota(jnp.int32, (H_RET, RET_WIDTH), 0)
    hlane = lax.broadcasted_iota(jnp.int32, (H_RET, RET_WIDTH), 1) // DK_RET
    own = hrow == hlane

    qm = jnp.where(own, qr_ref[0], 0.0)
    km = jnp.where(own, kr_ref[0], 0.0)
    v8 = vr_ref[0]
    g8 = gr_ref[0]
    st = st_ref[0]
    lg8 = _select_log_gamma(lax.broadcasted_iota(jnp.int32, (H_RET, 1), 0))
    cross = jnp.dot(qm.astype(BF16), st.astype(BF16), preferred_element_type=F32) * jnp.exp(lg8)
    qk = jnp.sum(qm * km, axis=-1, keepdims=True)
    o = qk * v8 + cross
    lg_rows = _select_log_gamma(lax.broadcasted_iota(jnp.int32, (H_RET * DK_RET, 1), 0) // DK_RET)
    outer = lax.dot_general(km, v8, (((0,), (0,)), ((), ())), preferred_element_type=F32,
                            precision=lax.Precision.HIGHEST)
    nst_ref[0] = jnp.exp(lg_rows) * st + outer
    mu = jnp.mean(o, axis=-1, keepdims=True)
    xc = o - mu
    var = jnp.mean(xc * xc, axis=-1, keepdims=True)
    r_ref[0] = xc * lax.rsqrt(var + LN_EPS) * (g8 * _sigmoid(g8))

    n_past = kt_ref.shape[-1]
    s_new = jnp.sum(qa_ref[0] * ka_ref[0], axis=-1, keepdims=True) * ATT_SCALE
    qt = qat_ref[0] * ATT_SCALE
    for h in range(H_ATT):
        s_s[h:h + 1, :] = jnp.sum(kt_ref[h] * qt[:, h:h + 1], axis=0, keepdims=True)
    tok = lax.broadcasted_iota(jnp.int32, (H_ATT, n_past), 1)
    stats = []
    for window, dil in DILATIONS:
        lo = n_past - window
        sb = s_s[:, lo:] + jnp.where(tok[:, lo:] % dil == n_past % dil, 0.0, NEG_BIG)
        m = jnp.maximum(jnp.max(sb, axis=-1, keepdims=True), s_new)
        p = jnp.exp(sb - m)
        p_new = jnp.exp(s_new - m)
        den = jnp.sum(p, axis=-1, keepdims=True) + p_new
        stats.append((lo, p, p_new, den, m + jnp.log(den)))
    mx = jnp.maximum(jnp.maximum(stats[0][4], stats[1][4]), stats[2][4])
    es = [jnp.exp(st_[4] - mx) for st_ in stats]
    tot = es[0] + es[1] + es[2]
    coef = [e / (tot * st_[3]) for e, st_ in zip(es, stats)]
    w_new = coef[0] * stats[0][2] + coef[1] * stats[1][2] + coef[2] * stats[2][2]
    order = sorted(range(len(stats)), key=lambda i: stats[i][0])
    assert stats[order[0]][0] == 0
    pc_s[...] = coef[order[0]] * stats[order[0]][1]
    for i in order[1:]:
        lo = stats[i][0]
        pc_s[:, lo:] = pc_s[:, lo:] + coef[i] * stats[i][1]
    vt_new = vat_ref[0]
    for h in range(H_ATT):
        col = jnp.sum(vt_ref[h] * pc_s[h:h + 1, :], axis=-1, keepdims=True)
        att_ref[0, :, h:h + 1] = col + w_new[h:h + 1, :] * vt_new[:, h:h + 1]


def _sample_mix_call(qr, kr, vr, gr, state, qa, ka, va, cache_k, cache_v):
    nb = qr.shape[0]
    n_past = cache_k.shape[-1]
    assert all(w <= n_past and n_past % d == 0 for w, d in DILATIONS) and max(w for w, _ in DILATIONS) == n_past
    row3 = pl.BlockSpec((1, 1, RET_WIDTH), lambda b: (b, 0, 0))
    head3 = pl.BlockSpec((1, H_RET, DK_RET), lambda b: (b, 0, 0))
    col3 = pl.BlockSpec((1, HD_ATT, H_ATT), lambda b: (b, 0, 0))
    st_spec = pl.BlockSpec((1, H_RET * DK_RET, DK_RET), lambda b: (b, 0, 0))
    cache_spec = pl.BlockSpec((None, H_ATT, HD_ATT, n_past), lambda b: (b, 0, 0, 0))
    r3 = lambda a: a.reshape(nb, 1, RET_WIDTH)
    h3 = lambda a: a.reshape(nb, H_RET, DK_RET)
    t3 = lambda a: jnp.swapaxes(h3(a), 1, 2)
    r_s, att_t, nst = pl.pallas_call(
        _sample_mix_kernel,
        grid=(nb,),
        in_specs=[row3, row3, head3, head3, st_spec, head3, head3, col3, col3, cache_spec, cache_spec],
        out_specs=[head3, col3, st_spec],
        out_shape=[jax.ShapeDtypeStruct((nb, H_RET, DK_RET), F32),
                   jax.ShapeDtypeStruct((nb, HD_ATT, H_ATT), F32),
                   jax.ShapeDtypeStruct((nb, H_RET * DK_RET, DK_RET), F32)],
        scratch_shapes=[pltpu.VMEM((H_ATT, n_past), F32), pltpu.VMEM((H_ATT, n_past), F32)],
        compiler_params=pltpu.CompilerParams(dimension_semantics=("parallel",), vmem_limit_bytes=VMEM_LIMIT),
    )(r3(qr), r3(kr), h3(vr), h3(gr), state.reshape(nb, H_RET * DK_RET, DK_RET), h3(qa), h3(ka), t3(qa), t3(va),
      cache_k, cache_v)
    return r_s, jnp.swapaxes(att_t, 1, 2), nst


def kernel(x_prompt, x_sample, cache_k_win, cache_v_win, state_ret, state_conv, p_prompt, p_sample,
           ln_in_g, ln_in_b, w_in, w_out, ln1_g, ln1_b, w_up, conv_w, conv_b, w_down,
           w_ple_gate, w_ple_proj, ln2_g, ln2_b):
    B, S, _ = x_prompt.shape
    NB, T, _ = x_sample.shape
    assert T == 1 and w_in.shape[0] == DEPTH == 1 and S % PROJ_TILE == 0 and S % POST_TILE == 0
    two_f = 2 * D_FF
    vec = lambda a: a.reshape(1, -1)
    w_in_b = w_in[0].astype(BF16)
    post_w = (vec(ln_in_g), vec(ln_in_b), w_out[0].astype(BF16), vec(ln1_g[0]), vec(ln1_b[0]),
              w_up[0].astype(BF16), conv_w[0], vec(conv_b[0]), w_down[0].astype(BF16),
              w_ple_gate[0].astype(BF16), w_ple_proj[0].astype(BF16), vec(ln2_g[0]), vec(ln2_b[0]))

    pos_p = jnp.arange(S, dtype=jnp.int32)
    tabs_p = _rot_tables(pos_p, DK_RET, RET_ROPE_BASE) + _rot_tables(pos_p, ROT_DIMS, ATT_ROPE_THETA)
    xp = x_prompt.reshape(B * S, D_MODEL)
    qr, kr, vr, gr, qa, ka, va, ka_t, va_t = _proj_call(
        xp, vec(ln_in_g), vec(ln_in_b), w_in_b, tabs_p, PROJ_TILE, S // PROJ_TILE,
        (BF16, F32, BF16, F32, F32, F32, F32), seq_for_transposed=S)
    r_p, ret_fin, att_p = _mixer_call(qr, kr, vr, gr, qa, ka, va, B, S)
    y_p, conv_p = _post_call(xp, r_p, att_p, p_prompt[0].reshape(B * S, PLE_DIM), post_w,
                             POST_TILE, S // POST_TILE)
    keep = min(WINDOW_MAX, S)
    k_win_p = jnp.transpose(ka_t, (0, 3, 1, 2))[:, S - keep:]
    v_win_p = jnp.transpose(va_t, (0, 3, 1, 2))[:, S - keep:]

    pos_s = jnp.full((NB,), PAST_LEN, jnp.int32)
    tabs_s = _rot_tables(pos_s, DK_RET, RET_ROPE_BASE) + _rot_tables(pos_s, ROT_DIMS, ATT_ROPE_THETA)
    xs = x_sample.reshape(NB, D_MODEL)
    sqr, skr, svr, sgr, sqa, ska, sva = _proj_call(xs, vec(ln_in_g), vec(ln_in_b), w_in_b, tabs_s, NB, 1,
                                                   (F32,) * 7)
    r_s, att_s, nst = _sample_mix_call(sqr, skr, svr, sgr, state_ret[0], sqa, ska, sva,
                                       jnp.transpose(cache_k_win[0], (0, 2, 3, 1)),
                                       jnp.transpose(cache_v_win[0], (0, 2, 3, 1)))
    y_s, u_s = _post_call(xs, r_s.reshape(NB, RET_WIDTH), att_s.reshape(NB, ATT_WIDTH), p_sample[0].reshape(NB, PLE_DIM),
                          post_w, NB, 1, prefix=(state_conv[0][:, 0], state_conv[0][:, 1]))
    conv_s = jnp.stack([state_conv[0][:, 1], u_s], axis=1)

    return (y_p.reshape(B, S, D_MODEL), y_s.reshape(NB, 1, D_MODEL),
            k_win_p[None], v_win_p[None], ret_fin[None], conv_p[None],
            ska.reshape(1, NB, 1, H_ATT, HD_ATT), sva.reshape(1, NB, 1, H_ATT, HD_ATT),
            nst.reshape(1, NB, H_RET, DK_RET, DK_RET), conv_s[None])
```
